```python
import jax, jax.numpy as jnp
from jax import lax
import numpy as np

D_MODEL = 1024
BATCH = 1
SEQ = 16384
DEPTH = 4

ATT_HEADS = 8
ATT_KV_HEADS = 2
HEAD_DIM = 64
WINDOW = 128
ATT_BLOCK = 128
ATT_WIDTH = ATT_HEADS * HEAD_DIM
KV_WIDTH = ATT_KV_HEADS * HEAD_DIM
CONV_WIDTH = 512
CONV_K = 3
POOL_SIZES = (2, 4, 8, 16)
POOL_GROUP = 128
POOL_WIDTH = 512
SGU_GROUPS = 4
SGU_CHUNK = 128
SGU_WIDTH = 512
SGU_GROUP = SGU_WIDTH // SGU_GROUPS
N_BRANCH = 4
BRANCH_WIDTH = 512
IN_COLS = ATT_WIDTH + 2 * KV_WIDTH + 3 * CONV_WIDTH + POOL_WIDTH + 2 * SGU_WIDTH + N_BRANCH * D_MODEL
FFN_HIDDEN = 2816
N_EXPERTS = 8
TOP_K = 2
N_DENSE = (DEPTH + 1) // 2
N_MOE = DEPTH // 2
RMS_EPS = 1e-6
NEG_INF = -1e30

kernel_name = "hybrid_gated_multimixer_moe_trunk"


def rmsnorm(x, g):
    xf = x.astype(jnp.float32)
    y = xf * lax.rsqrt(jnp.mean(xf * xf, axis=-1, keepdims=True) + RMS_EPS)
    return (y * g.astype(jnp.float32)).astype(x.dtype)


def sliding_window_attention(q, k, v, sinks):
    b, s, _ = q.shape
    nb = s // ATT_BLOCK
    grp = ATT_HEADS // ATT_KV_HEADS
    qb = q.reshape(b, nb, ATT_BLOCK, ATT_KV_HEADS, grp, HEAD_DIM)

    def band(t):
        t = t.reshape(b, nb, ATT_BLOCK, ATT_KV_HEADS, HEAD_DIM)
        prev = jnp.pad(t, ((0, 0), (1, 0), (0, 0), (0, 0), (0, 0)))[:, :-1]
        return jnp.concatenate([prev, t], axis=2)

    kb, vb = band(k), band(v)
    scores = jnp.einsum('bnqkgd,bnskd->bnkgqs', qb, kb).astype(jnp.float32) * (HEAD_DIM ** -0.5)
    qi = jnp.arange(ATT_BLOCK)[:, None]
    kj = jnp.arange(2 * ATT_BLOCK)[None, :]
    rel = qi + ATT_BLOCK - kj
    local = (rel >= 0) & (rel < WINDOW)
    kpos = jnp.arange(nb)[:, None] * ATT_BLOCK - ATT_BLOCK + kj
    valid = local[None] & (kpos >= 0)[:, None, :]
    scores = jnp.where(valid[None, :, None, None], scores, NEG_INF)
    sink_col = jnp.broadcast_to(
        sinks.astype(jnp.float32).reshape(1, 1, ATT_KV_HEADS, grp, 1, 1), scores.shape[:-1] + (1,))
    probs = jax.nn.softmax(jnp.concatenate([scores, sink_col], axis=-1), axis=-1)[..., :-1]
    out = jnp.einsum('bnkgqs,bnskd->bnqkgd', probs.astype(v.dtype), vb)
    return out.reshape(b, s, ATT_WIDTH)


def short_conv_mixer(bgate, cgate, xin, conv_w):
    z = cgate * xin
    s = z.shape[1]
    zp = jnp.pad(z, ((0, 0), (CONV_K - 1, 0), (0, 0)))
    y = sum(conv_w[j] * zp[:, j:j + s] for j in range(CONV_K))
    return bgate * y


def multiscale_pool_mixer(p, pool_w, pool_scale):
    b, s, _ = p.shape
    pf = p.astype(jnp.float32)
    cs = jnp.pad(jnp.cumsum(pf, axis=1), ((0, 0), (1, 0), (0, 0)))
    t = jnp.arange(s)
    outs = []
    for g, w in enumerate(POOL_SIZES):
        sl = slice(g * POOL_GROUP, (g + 1) * POOL_GROUP)
        start = jnp.maximum(t + 1 - w, 0)
        cnt = (t + 1 - start).astype(jnp.float32)
        win_sum = cs[:, 1:, sl] - cs[:, start, sl]
        outs.append(win_sum / cnt[None, :, None] - pf[..., sl])
    pooled = jnp.stack(outs, axis=2).astype(p.dtype)
    y = jnp.einsum('bsgc,gcd->bsgd', pooled, pool_w).reshape(b, s, POOL_WIDTH)
    return y * pool_scale


def spatial_gating_mixer(u, v, sgu_w, sgu_b):
    b, s, _ = u.shape
    nc = s // SGU_CHUNK
    mask = jnp.tril(jnp.ones((SGU_CHUNK, SGU_CHUNK), dtype=bool))
    ws = jnp.where(mask[None], sgu_w, jnp.zeros_like(sgu_w))
    vc = v.reshape(b, nc, SGU_CHUNK, SGU_GROUPS, SGU_GROUP)
    mixed = jnp.einsum('gts,bnsgc->bntgc', ws, vc) + sgu_b.T[None, None, :, :, None]
    return u * mixed.reshape(b, s, SGU_WIDTH)


def hybrid_mixer(h, w_in, sinks, conv_w, pool_w, pool_scale, sgu_w, sgu_b, w_branch, w_out):
    b, s, _ = h.shape
    proj = h @ w_in
    sizes = [ATT_WIDTH, KV_WIDTH, KV_WIDTH, CONV_WIDTH, CONV_WIDTH, CONV_WIDTH,
             POOL_WIDTH, SGU_WIDTH, SGU_WIDTH, N_BRANCH * D_MODEL]
    splits = [int(c) for c in np.cumsum(sizes)[:-1]]
    q, k, v, cb, cc, cx, p, u, sv, g = jnp.split(proj, splits, axis=-1)
    y_att = sliding_window_attention(q, k, v, sinks)
    y_conv = short_conv_mixer(cb, cc, cx, conv_w)
    y_pool = multiscale_pool_mixer(p, pool_w, pool_scale)
    y_sgu = spatial_gating_mixer(u, sv, sgu_w, sgu_b)
    branches = jnp.stack([y_att, y_conv, y_pool, y_sgu], axis=2)
    up = jnp.einsum('bsnc,ncd->bsnd', branches, w_branch)
    gates = jax.nn.sigmoid(g.reshape(b, s, N_BRANCH, D_MODEL))
    merged = jnp.sum(gates * up, axis=2)
    return merged @ w_out


def swiglu(h, w_gate, w_up, w_down):
    return (jax.nn.silu(h @ w_gate) * (h @ w_up)) @ w_down


def moe_swiglu(h, router_w, router_b, w_gate, w_up, w_down):
    logits = (h @ router_w).astype(jnp.float32) + router_b.astype(jnp.float32)
    top_val, top_idx = lax.top_k(logits, TOP_K)
    top_w = jax.nn.softmax(top_val, axis=-1)
    combine = jnp.sum(jax.nn.one_hot(top_idx, N_EXPERTS, dtype=jnp.float32) * top_w[..., None], axis=-2)
    y = jnp.zeros_like(h)
    for e in range(N_EXPERTS):
        y = y + combine[..., e:e + 1].astype(h.dtype) * swiglu(h, w_gate[e], w_up[e], w_down[e])
    return y


def setup_inputs(seed: int = 0) -> dict:
    key = jax.random.key(seed)
    ks = jax.random.split(key, 24)
    nrm = lambda k, shape, scale: jax.random.normal(k, shape, jnp.float32) * scale
    gain = lambda k, shape: 1.0 + 0.05 * jax.random.normal(k, shape, jnp.float32)
    return {
        "x": nrm(ks[0], (BATCH, SEQ, D_MODEL), 1.0),
        "ln_mix_g": gain(ks[1], (DEPTH, D_MODEL)),
        "w_in": nrm(ks[2], (DEPTH, D_MODEL, IN_COLS), D_MODEL ** -0.5),
        "attn_sinks": nrm(ks[3], (DEPTH, ATT_HEADS), 1.0),
        "conv_w": nrm(ks[4], (DEPTH, CONV_K, CONV_WIDTH), CONV_K ** -0.5),
        "pool_w": nrm(ks[5], (DEPTH, len(POOL_SIZES), POOL_GROUP, POOL_GROUP), POOL_GROUP ** -0.5),
        "pool_scale": gain(ks[6], (DEPTH, POOL_WIDTH)),
        "sgu_w": nrm(ks[7], (DEPTH, SGU_GROUPS, SGU_CHUNK, SGU_CHUNK), SGU_CHUNK ** -0.5),
        "sgu_b": gain(ks[8], (DEPTH, SGU_GROUPS, SGU_CHUNK)),
        "w_branch": nrm(ks[9], (DEPTH, N_BRANCH, BRANCH_WIDTH, D_MODEL), BRANCH_WIDTH ** -0.5),
        "w_out": nrm(ks[10], (DEPTH, D_MODEL, D_MODEL), D_MODEL ** -0.5),
        "ln_ffn_g": gain(ks[11], (DEPTH, D_MODEL)),
        "dense_w_gate": nrm(ks[12], (N_DENSE, D_MODEL, FFN_HIDDEN), D_MODEL ** -0.5),
        "dense_w_up": nrm(ks[13], (N_DENSE, D_MODEL, FFN_HIDDEN), D_MODEL ** -0.5),
        "dense_w_down": nrm(ks[14], (N_DENSE, FFN_HIDDEN, D_MODEL), FFN_HIDDEN ** -0.5),
        "router_w": nrm(ks[15], (N_MOE, D_MODEL, N_EXPERTS), D_MODEL ** -0.5),
        "router_b": nrm(ks[16], (N_MOE, N_EXPERTS), 0.01),
        "moe_w_gate": nrm(ks[17], (N_MOE, N_EXPERTS, D_MODEL, FFN_HIDDEN), D_MODEL ** -0.5),
        "moe_w_up": nrm(ks[18], (N_MOE, N_EXPERTS, D_MODEL, FFN_HIDDEN), D_MODEL ** -0.5),
        "moe_w_down": nrm(ks[19], (N_MOE, N_EXPERTS, FFN_HIDDEN, D_MODEL), FFN_HIDDEN ** -0.5),
        "ln_final_g": gain(ks[20], (D_MODEL,)),
    }


def reference(x, ln_mix_g, w_in, attn_sinks, conv_w, pool_w, pool_scale, sgu_w, sgu_b, w_branch, w_out,
              ln_ffn_g, dense_w_gate, dense_w_up, dense_w_down, router_w, router_b,
              moe_w_gate, moe_w_up, moe_w_down, ln_final_g):
    for layer in range(DEPTH):
        h = rmsnorm(x, ln_mix_g[layer])
        x = x + hybrid_mixer(h, w_in[layer], attn_sinks[layer], conv_w[layer], pool_w[layer],
                             pool_scale[layer], sgu_w[layer], sgu_b[layer], w_branch[layer], w_out[layer])
        h = rmsnorm(x, ln_ffn_g[layer])
        i = layer // 2
        if layer % 2 == 0:
            x = x + swiglu(h, dense_w_gate[i], dense_w_up[i], dense_w_down[i])
        else:
            x = x + moe_swiglu(h, router_w[i], router_b[i], moe_w_gate[i], moe_w_up[i], moe_w_down[i])
    return rmsnorm(x, ln_final_g)
```

```python
import functools

import jax
import jax.numpy as jnp
from jax import lax
from jax.experimental import pallas as pl
from jax.experimental.pallas import tpu as pltpu

F32 = jnp.float32
BF16 = jnp.bfloat16

D_MODEL = 1024
ATT_HEADS = 8
ATT_KV_HEADS = 2
HEAD_DIM = 64
ATT_BLOCK = 128
POOL_SIZES = (2, 4, 8, 16)
GROUP = 128
N_BRANCH = 4
BRANCH_WIDTH = 512
FFN_HIDDEN = 2816
N_EXPERTS = 8
RMS_EPS = 1e-6
NEG_INF = -1e30

_C_Q, _C_KV, _C_CONV, _C_POOL, _C_SGU, _C_GATE = 0, 512, 768, 2304, 2816, 3840
IN_COLS = _C_GATE + N_BRANCH * D_MODEL

LANES = 128
SUBLANES = 8
MXU_COLS = 256
VMEM_LIMIT_BYTES = 60 * 1024 * 1024

MIX_TM = 512
FFN_TM = 512
FFN_FC = MXU_COLS
ROUTER_TM = 512
GATHER_T = 256
CONV_HALO = SUBLANES
POOL_HALO = 2 * SUBLANES


def _rmsnorm(x, g):
    return x * lax.rsqrt(jnp.mean(x * x, axis=-1, keepdims=True) + RMS_EPS) * g


def _sigmoid(x):
    return 1.0 / (1.0 + jnp.exp(-x))


def _dot(a, b):
    return jnp.dot(a, b, preferred_element_type=F32)


def _const_spec(*shape):
    return pl.BlockSpec(shape, lambda *_: (0,) * len(shape), pipeline_mode=pl.Buffered(1))


def _mixer_body(sinks_ref, x_ref, g_ref, w_in_ref, conv_w_ref, pool_w_ref, pool_scale_ref,
                sgu_w_ref, sgu_b_ref, w_br_ref, w_out_ref, o_ref,
                kv_buf, z_buf, p_buf, y_buf, merged_buf):
    tm = x_ref.shape[0]
    nblk = tm // ATT_BLOCK
    i = pl.program_id(0)

    @pl.when(i == 0)
    def _():
        kv_buf[pl.ds(0, ATT_BLOCK), :] = jnp.zeros((ATT_BLOCK, kv_buf.shape[1]), F32)
        z_buf[pl.ds(0, CONV_HALO), :] = jnp.zeros((CONV_HALO, z_buf.shape[1]), F32)
        p_buf[pl.ds(0, POOL_HALO), :] = jnp.zeros((POOL_HALO, p_buf.shape[1]), F32)

    x = x_ref[...]
    hb = _rmsnorm(x, g_ref[...]).astype(BF16)

    def proj(c0, n):
        return _dot(hb, w_in_ref[:, c0:c0 + n])

    def gated_up(n):
        gate = _sigmoid(proj(_C_GATE + n * D_MODEL, D_MODEL))
        return gate * _dot(y_buf[...], w_br_ref[n])

    qkv = proj(_C_Q, 768)
    kv_buf[pl.ds(ATT_BLOCK, tm), :] = qkv[:, 512:768]
    q = qkv[:, 0:512] * (HEAD_DIM ** -0.5)
    lo128 = lax.broadcasted_iota(jnp.int32, (ATT_BLOCK, LANES), 1) < HEAD_DIM
    lo256 = lax.broadcasted_iota(jnp.int32, (2 * ATT_BLOCK, LANES), 1) < HEAD_DIM
    qi = lax.broadcasted_iota(jnp.int32, (ATT_BLOCK, 2 * ATT_BLOCK), 0)
    kj = lax.broadcasted_iota(jnp.int32, (ATT_BLOCK, 2 * ATT_BLOCK), 1)
    rel = qi + ATT_BLOCK - kj
    local = (rel >= 0) & (rel < ATT_BLOCK)
    for b in range(nblk):
        kvb = kv_buf[pl.ds(b * ATT_BLOCK, 2 * ATT_BLOCK), :]
        kb, vb = kvb[:, 0:LANES], kvb[:, LANES:2 * LANES]
        kr, vr = pltpu.roll(kb, HEAD_DIM, 1), pltpu.roll(vb, HEAD_DIM, 1)
        kkT = [jnp.where(lo256, kb, kr).T.astype(BF16), jnp.where(lo256, kr, kb).T.astype(BF16)]
        vv = [jnp.where(lo256, vb, vr).astype(BF16), jnp.where(lo256, vr, vb).astype(BF16)]
        valid = local & (kj + (i * tm + (b - 1) * ATT_BLOCK) >= 0)
        for j in range(ATT_HEADS // 2):
            qp = q[b * ATT_BLOCK:(b + 1) * ATT_BLOCK, j * LANES:(j + 1) * LANES]
            kvh = (2 * j) // (ATT_HEADS // ATT_KV_HEADS)
            outs = []
            for half in range(2):
                keep = lo128 if half == 0 else jnp.logical_not(lo128)
                qm = jnp.where(keep, qp, 0.0).astype(BF16)
                s = jnp.where(valid, _dot(qm, kkT[kvh]), NEG_INF)
                sink = sinks_ref[2 * j + half]
                m = jnp.maximum(jnp.max(s, axis=-1, keepdims=True), sink)
                p = jnp.exp(s - m)
                den = jnp.sum(p, axis=-1, keepdims=True) + jnp.exp(sink - m)
                outs.append(_dot(p.astype(BF16), vv[kvh]) / den)
            y_buf[pl.ds(b * ATT_BLOCK, ATT_BLOCK), j * LANES:(j + 1) * LANES] = (
                jnp.where(lo128, outs[0], outs[1]).astype(BF16))
    kv_buf[pl.ds(0, ATT_BLOCK), :] = kv_buf[pl.ds(tm, ATT_BLOCK), :]
    merged_buf[...] = gated_up(0)

    c3 = proj(_C_CONV, 3 * BRANCH_WIDTH)
    z = c3[:, 512:1024] * c3[:, 1024:1536]
    z_buf[pl.ds(CONV_HALO, tm), :] = z
    cw = conv_w_ref[...]
    conv = (cw[0:1] * z_buf[pl.ds(CONV_HALO - 2, tm), :] + cw[1:2] * z_buf[pl.ds(CONV_HALO - 1, tm), :]
            + cw[2:3] * z)
    z_buf[pl.ds(0, CONV_HALO), :] = z_buf[pl.ds(tm, CONV_HALO), :]
    y_buf[...] = (c3[:, 0:512] * conv).astype(BF16)
    merged_buf[...] += gated_up(1)

    pp = proj(_C_POOL, BRANCH_WIDTH)
    p_buf[pl.ds(POOL_HALO, tm), :] = pp
    tpos = i * tm + lax.broadcasted_iota(jnp.int32, (tm, GROUP), 0)
    for gi, w in enumerate(POOL_SIZES):
        cs = slice(gi * GROUP, (gi + 1) * GROUP)
        win = pp[:, cs]
        for k in range(1, w):
            win = win + p_buf[pl.ds(POOL_HALO - k, tm), cs]
        cnt = jnp.minimum(tpos + 1, w).astype(F32)
        pooled = win / cnt - pp[:, cs]
        yg = _dot(pooled.astype(BF16), pool_w_ref[gi]) * pool_scale_ref[:, cs]
        y_buf[:, cs] = yg.astype(BF16)
    p_buf[pl.ds(0, POOL_HALO), :] = p_buf[pl.ds(tm, POOL_HALO), :]
    merged_buf[...] += gated_up(2)

    usv = proj(_C_SGU, 2 * BRANCH_WIDTH)
    tril = (lax.broadcasted_iota(jnp.int32, (GROUP, GROUP), 0)
            >= lax.broadcasted_iota(jnp.int32, (GROUP, GROUP), 1))
    for g in range(BRANCH_WIDTH // GROUP):
        wg = jnp.where(tril, sgu_w_ref[g], jnp.zeros((GROUP, GROUP), BF16))
        for c in range(nblk):
            rows = slice(c * GROUP, (c + 1) * GROUP)
            sv = usv[rows, BRANCH_WIDTH + g * GROUP:BRANCH_WIDTH + (g + 1) * GROUP].astype(BF16)
            mixed = _dot(wg, sv) + sgu_b_ref[g]
            y_buf[pl.ds(c * GROUP, GROUP), g * GROUP:(g + 1) * GROUP] = (
                usv[rows, g * GROUP:(g + 1) * GROUP] * mixed).astype(BF16)
    merged_buf[...] += gated_up(3)

    o_ref[...] = x + _dot(merged_buf[...].astype(BF16), w_out_ref[...])


def _mixer(x, ln_g, w_in, sinks, conv_w, pool_w, pool_scale, sgu_w, sgu_b_full, w_br, w_out):
    s = x.shape[0]
    tm = MIX_TM
    return pl.pallas_call(
        _mixer_body,
        grid=(s // tm,),
        in_specs=[
            pl.BlockSpec(memory_space=pltpu.SMEM),
            pl.BlockSpec((tm, D_MODEL), lambda i: (i, 0)),
            _const_spec(1, D_MODEL),
            _const_spec(D_MODEL, IN_COLS),
            _const_spec(3, BRANCH_WIDTH),
            _const_spec(4, GROUP, GROUP),
            _const_spec(1, BRANCH_WIDTH),
            _const_spec(4, GROUP, GROUP),
            _const_spec(4, GROUP, GROUP),
            _const_spec(N_BRANCH, BRANCH_WIDTH, D_MODEL),
            _const_spec(D_MODEL, D_MODEL),
        ],
        out_specs=pl.BlockSpec((tm, D_MODEL), lambda i: (i, 0)),
        out_shape=jax.ShapeDtypeStruct((s, D_MODEL), F32),
        scratch_shapes=[
            pltpu.VMEM((tm + ATT_BLOCK, 2 * LANES), F32),
            pltpu.VMEM((tm + CONV_HALO, BRANCH_WIDTH), F32),
            pltpu.VMEM((tm + POOL_HALO, BRANCH_WIDTH), F32),
            pltpu.VMEM((tm, BRANCH_WIDTH), BF16),
            pltpu.VMEM((tm, D_MODEL), F32),
        ],
        compiler_params=pltpu.CompilerParams(
            dimension_semantics=("arbitrary",), vmem_limit_bytes=VMEM_LIMIT_BYTES),
        name="mixer",
    )(sinks, x, ln_g, w_in, conv_w, pool_w, pool_scale, sgu_w, sgu_b_full, w_br, w_out)


def _swiglu_into(hb, wg_ref, wu_ref, wd_ref, acc_ref, base):
    for c in range(FFN_HIDDEN // FFN_FC):
        cs = slice(c * FFN_FC, (c + 1) * FFN_FC)
        g = _dot(hb, wg_ref[:, cs])
        u = _dot(hb, wu_ref[:, cs])
        a = (g * _sigmoid(g) * u).astype(BF16)
        d = _dot(a, wd_ref[cs, :])
        if c == 0:
            acc_ref[...] = d if base is None else base + d
        else:
            acc_ref[...] += d


def _dense_ffn_body(x_ref, g_ref, wg_ref, wu_ref, wd_ref, o_ref):
    x = x_ref[...]
    hb = _rmsnorm(x, g_ref[...]).astype(BF16)
    _swiglu_into(hb, wg_ref, wu_ref, wd_ref, o_ref, x)


def _dense_ffn(x, ln_g, wg, wu, wd):
    s = x.shape[0]
    tm = FFN_TM
    return pl.pallas_call(
        _dense_ffn_body,
        grid=(s // tm,),
        in_specs=[
            pl.BlockSpec((tm, D_MODEL), lambda i: (i, 0)),
            _const_spec(1, D_MODEL),
            _const_spec(D_MODEL, FFN_HIDDEN),
            _const_spec(D_MODEL, FFN_HIDDEN),
            _const_spec(FFN_HIDDEN, D_MODEL),
        ],
        out_specs=pl.BlockSpec((tm, D_MODEL), lambda i: (i, 0)),
        out_shape=jax.ShapeDtypeStruct((s, D_MODEL), F32),
        compiler_params=pltpu.CompilerParams(
            dimension_semantics=("arbitrary",), vmem_limit_bytes=VMEM_LIMIT_BYTES),
        name="dense_ffn",
    )(x, ln_g, wg, wu, wd)


def _expert_ffn_body(tile_expert_ref, n_used_ref, hs_ref, wg_ref, wu_ref, wd_ref, o_ref):
    del tile_expert_ref
    i = pl.program_id(0)

    @pl.when(i < n_used_ref[0])
    def _():
        _swiglu_into(hs_ref[...].astype(BF16), wg_ref, wu_ref, wd_ref, o_ref, None)

    @pl.when(i >= n_used_ref[0])
    def _():
        o_ref[...] = jnp.zeros(o_ref.shape, F32)


def _expert_ffn(tile_expert, n_used, hs, wg, wu, wd):
    rows = hs.shape[0]
    tm = FFN_TM
    w_in_spec = pl.BlockSpec((None, D_MODEL, FFN_HIDDEN), lambda i, te, nu: (te[i], 0, 0))
    w_out_spec = pl.BlockSpec((None, FFN_HIDDEN, D_MODEL), lambda i, te, nu: (te[i], 0, 0))
    return pl.pallas_call(
        _expert_ffn_body,
        grid_spec=pltpu.PrefetchScalarGridSpec(
            num_scalar_prefetch=2,
            grid=(rows // tm,),
            in_specs=[
                pl.BlockSpec((tm, D_MODEL), lambda i, te, nu: (jnp.minimum(i, nu[0] - 1), 0)),
                w_in_spec, w_in_spec, w_out_spec,
            ],
            out_specs=pl.BlockSpec((tm, D_MODEL), lambda i, te, nu: (i, 0)),
        ),
        out_shape=jax.ShapeDtypeStruct((rows, D_MODEL), F32),
        compiler_params=pltpu.CompilerParams(
            dimension_semantics=("arbitrary",), vmem_limit_bytes=VMEM_LIMIT_BYTES),
        name="expert_ffn",
    )(tile_expert, n_used, hs, wg, wu, wd)


def _router_body(x_ref, g_ref, rw_ref, rb_ref, h_ref, meta_ref, cnt_ref, carry):
    tm = x_ref.shape[0]
    i = pl.program_id(0)

    @pl.when(i == 0)
    def _():
        carry[...] = jnp.zeros(carry.shape, F32)

    h = _rmsnorm(x_ref[...], g_ref[...])
    h_ref[...] = h
    h_hi = h.astype(BF16)
    h_lo = (h - h_hi.astype(F32)).astype(BF16)
    w = rw_ref[...]
    w_hi = w.astype(BF16)
    w_lo = (w - w_hi.astype(F32)).astype(BF16)
    logits = _dot(h_hi, w_hi) + (_dot(h_lo, w_hi) + _dot(h_hi, w_lo)) + rb_ref[...]
    lane = lax.broadcasted_iota(jnp.int32, (tm, LANES), 1)
    logits = jnp.where(lane < N_EXPERTS, logits, NEG_INF)
    m1 = jnp.max(logits, axis=-1, keepdims=True)
    i1 = jnp.min(jnp.where(logits == m1, lane, LANES), axis=-1, keepdims=True)
    rest = jnp.where(lane == i1, NEG_INF, logits)
    m2 = jnp.max(rest, axis=-1, keepdims=True)
    i2 = jnp.min(jnp.where(rest == m2, lane, LANES), axis=-1, keepdims=True)
    e = jnp.exp(m2 - m1)
    w1 = 1.0 / (1.0 + e)
    w2 = e / (1.0 + e)
    oh1, oh2 = lane == i1, lane == i2
    cnt = jnp.where(oh1 | oh2, 1.0, 0.0)
    below = (lax.broadcasted_iota(jnp.int32, (tm, tm), 0) > lax.broadcasted_iota(jnp.int32, (tm, tm), 1))
    prefix = _dot(jnp.where(below, 1.0, 0.0).astype(BF16), cnt.astype(BF16)) + carry[0:1, :]
    r1 = jnp.sum(jnp.where(oh1, prefix, 0.0), axis=-1, keepdims=True)
    r2 = jnp.sum(jnp.where(oh2, prefix, 0.0), axis=-1, keepdims=True)
    meta = jnp.zeros((tm, LANES), F32)
    for k, v in enumerate((i1.astype(F32), i2.astype(F32), w1, w2, r1, r2)):
        meta = jnp.where(lane == k, v, meta)
    meta_ref[...] = meta
    carry[...] += jnp.sum(cnt, axis=0, keepdims=True)
    cnt_ref[...] = carry[...]


def _router(x, ln_g, rw_pad, rb_pad):
    s = x.shape[0]
    tm = ROUTER_TM
    return pl.pallas_call(
        _router_body,
        grid=(s // tm,),
        in_specs=[
            pl.BlockSpec((tm, D_MODEL), lambda i: (i, 0)),
            _const_spec(1, D_MODEL),
            _const_spec(D_MODEL, LANES),
            _const_spec(1, LANES),
        ],
        out_specs=[
            pl.BlockSpec((tm, D_MODEL), lambda i: (i, 0)),
            pl.BlockSpec((tm, LANES), lambda i: (i, 0)),
            pl.BlockSpec((SUBLANES, LANES), lambda i: (0, 0)),
        ],
        out_shape=[
            jax.ShapeDtypeStruct((s, D_MODEL), F32),
            jax.ShapeDtypeStruct((s, LANES), F32),
            jax.ShapeDtypeStruct((SUBLANES, LANES), F32),
        ],
        scratch_shapes=[pltpu.VMEM((SUBLANES, LANES), F32)],
        compiler_params=pltpu.CompilerParams(
            dimension_semantics=("arbitrary",), vmem_limit_bytes=VMEM_LIMIT_BYTES),
        name="router",
    )(x, ln_g, rw_pad, rb_pad)


def _row_copy(src_ref, src_row, dst_ref, dst_row, sem):
    return pltpu.make_async_copy(src_ref.at[pl.ds(src_row, 1)], dst_ref.at[pl.ds(dst_row, 1)], sem)


def _gather_body(dest_ref, h_ref, hs_init_ref, hs_ref, sem):
    del hs_init_ref
    i = pl.program_id(0)

    def start(r, c):
        t = i * GATHER_T + r
        _row_copy(h_ref, t, hs_ref, dest_ref[2 * t], sem).start()
        _row_copy(h_ref, t, hs_ref, dest_ref[2 * t + 1], sem).start()
        return c

    lax.fori_loop(0, GATHER_T, start, 0, unroll=8)

    def wait(r, c):
        _row_copy(h_ref, 0, hs_ref, 0, sem).wait()
        return c

    lax.fori_loop(0, 2 * GATHER_T, wait, 0, unroll=8)


def _gather_rows(dest, h, rows):
    s = h.shape[0]
    return pl.pallas_call(
        _gather_body,
        grid_spec=pltpu.PrefetchScalarGridSpec(
            num_scalar_prefetch=1,
            grid=(s // GATHER_T,),
            in_specs=[pl.BlockSpec(memory_space=pl.ANY), pl.BlockSpec(memory_space=pl.ANY)],
            out_specs=pl.BlockSpec(memory_space=pl.ANY),
            scratch_shapes=[pltpu.SemaphoreType.DMA],
        ),
        out_shape=jax.ShapeDtypeStruct((rows, D_MODEL), F32),
        input_output_aliases={2: 0},
        compiler_params=pltpu.CompilerParams(dimension_semantics=("arbitrary",)),
        name="gather_rows",
    )(dest, h, jnp.zeros((rows, D_MODEL), F32))


def _combine_body(dest_ref, x_ref, meta_ref, ys_ref, *rest, final_norm):
    if final_norm:
        g_ref, o_ref, y_buf, sem = rest
    else:
        o_ref, y_buf, sem = rest
    i = pl.program_id(0)

    def start(r, c):
        t = i * GATHER_T + r
        _row_copy(ys_ref, dest_ref[2 * t], y_buf.at[0], r, sem).start()
        _row_copy(ys_ref, dest_ref[2 * t + 1], y_buf.at[1], r, sem).start()
        return c

    lax.fori_loop(0, GATHER_T, start, 0, unroll=8)

    def wait(r, c):
        _row_copy(ys_ref, 0, y_buf.at[0], 0, sem).wait()
        return c

    lax.fori_loop(0, 2 * GATHER_T, wait, 0, unroll=8)

    meta = meta_ref[...]
    lane = lax.broadcasted_iota(jnp.int32, meta.shape, 1)
    w1 = jnp.sum(jnp.where(lane == 2, meta, 0.0), axis=-1, keepdims=True)
    w2 = jnp.sum(jnp.where(lane == 3, meta, 0.0), axis=-1, keepdims=True)
    y = x_ref[...] + (w1 * y_buf[0] + w2 * y_buf[1])
    o_ref[...] = _rmsnorm(y, g_ref[...]) if final_norm else y


def _combine(dest, x, meta, ys, final_g=None):
    s = x.shape[0]
    t = GATHER_T
    in_specs = [
        pl.BlockSpec((t, D_MODEL), lambda i, d: (i, 0)),
        pl.BlockSpec((t, LANES), lambda i, d: (i, 0)),
        pl.BlockSpec(memory_space=pl.ANY),
    ]
    args = [dest, x, meta, ys]
    if final_g is not None:
        in_specs.append(pl.BlockSpec((1, D_MODEL), lambda i, d: (0, 0)))
        args.append(final_g)
    return pl.pallas_call(
        functools.partial(_combine_body, final_norm=final_g is not None),
        grid_spec=pltpu.PrefetchScalarGridSpec(
            num_scalar_prefetch=1,
            grid=(s // t,),
            in_specs=in_specs,
            out_specs=pl.BlockSpec((t, D_MODEL), lambda i, d: (i, 0)),
            scratch_shapes=[pltpu.VMEM((2, t, D_MODEL), F32), pltpu.SemaphoreType.DMA],
        ),
        out_shape=jax.ShapeDtypeStruct((s, D_MODEL), F32),
        compiler_params=pltpu.CompilerParams(dimension_semantics=("arbitrary",)),
        name="combine",
    )(*args)


def _moe_ffn(x, ln_g, router_w, router_b, wg, wu, wd, final_g):
    s = x.shape[0]
    tm = FFN_TM
    rw_pad = jnp.zeros((D_MODEL, LANES), F32).at[:, :N_EXPERTS].set(router_w)
    rb_pad = jnp.zeros((1, LANES), F32).at[0, :N_EXPERTS].set(router_b)
    h, meta, counts = _router(x, ln_g, rw_pad, rb_pad)
    idx = meta[:, 0:2].astype(jnp.int32)
    rank = meta[:, 4:6].astype(jnp.int32)
    cnt = counts[0, :N_EXPERTS].astype(jnp.int32)
    padded = (cnt + tm - 1) // tm * tm
    ends = jnp.cumsum(padded)
    dest = ((ends - padded)[idx] + rank).reshape(-1)
    n_tiles = 2 * s // tm + N_EXPERTS
    tile_start = jnp.arange(n_tiles, dtype=jnp.int32) * tm
    tile_expert = jnp.minimum(
        jnp.sum(tile_start[:, None] >= ends[None, :], axis=1), N_EXPERTS - 1).astype(jnp.int32)
    n_used = (ends[-1:] // tm).astype(jnp.int32)
    hs = _gather_rows(dest, h, n_tiles * tm)
    ys = _expert_ffn(tile_expert, n_used, hs, wg, wu, wd)
    return _combine(dest, x, meta, ys, final_g)


def kernel(x, ln_mix_g, w_in, attn_sinks, conv_w, pool_w, pool_scale, sgu_w, sgu_b, w_branch, w_out,
           ln_ffn_g, dense_w_gate, dense_w_up, dense_w_down, router_w, router_b,
           moe_w_gate, moe_w_up, moe_w_down, ln_final_g):
    batch, seq, _ = x.shape
    depth = w_in.shape[0]
    assert batch == 1 and seq % MIX_TM == 0 and seq % FFN_TM == 0 and depth % 2 == 0
    xs = x.reshape(seq, D_MODEL)
    for layer in range(depth):
        sgu_b_full = jnp.broadcast_to(sgu_b[layer][:, :, None], (4, GROUP, GROUP))
        xs = _mixer(xs, ln_mix_g[layer][None], w_in[layer].astype(BF16), attn_sinks[layer],
                    conv_w[layer], pool_w[layer].astype(BF16), pool_scale[layer][None],
                    sgu_w[layer].astype(BF16), sgu_b_full, w_branch[layer].astype(BF16),
                    w_out[layer].astype(BF16))
        i = layer // 2
        if layer % 2 == 0:
            xs = _dense_ffn(xs, ln_ffn_g[layer][None], dense_w_gate[i].astype(BF16),
                            dense_w_up[i].astype(BF16), dense_w_down[i].astype(BF16))
        else:
            final_g = ln_final_g[None] if layer == depth - 1 else None
            xs = _moe_ffn(xs, ln_ffn_g[layer][None], router_w[i], router_b[i],
                          moe_w_gate[i].astype(BF16), moe_w_up[i].astype(BF16),
                          moe_w_down[i].astype(BF16), final_g)
    return xs.reshape(batch, seq, D_MODEL)
```

```python
import functools

import jax
import jax.numpy as jnp
from jax import lax
from jax.experimental import pallas as pl
from jax.experimental.pallas import tpu as pltpu

F32 = jnp.float32
BF16 = jnp.bfloat16

D_MODEL = 1024
ATT_HEADS = 8
ATT_KV_HEADS = 2
HEAD_DIM = 64
ATT_BLOCK = 128
POOL_SIZES = (2, 4, 8, 16)
GROUP = 128
N_BRANCH = 4
BRANCH_WIDTH = 512
FFN_HIDDEN = 2816
N_EXPERTS = 8
RMS_EPS = 1e-6
NEG_INF = -1e30

_C_Q, _C_KV, _C_CONV, _C_POOL, _C_SGU, _C_GATE = 0, 512, 768, 2304, 2816, 3840
IN_COLS = _C_GATE + N_BRANCH * D_MODEL

LANES = 128
SUBLANES = 8
MXU_COLS = 256
VMEM_LIMIT_BYTES = 60 * 1024 * 1024

MIX_TM = 512
FFN_TM = 512
FFN_FC = MXU_COLS
ROUTER_TM = 512
GATHER_T = 256
CONV_HALO = SUBLANES
POOL_HALO = 2 * SUBLANES


def _rmsnorm(x, g):
    return x * lax.rsqrt(jnp.mean(x * x, axis=-1, keepdims=True) + RMS_EPS) * g


def _sigmoid(x):
    return 1.0 / (1.0 + jnp.exp(-x))


def _dot(a, b):
    return jnp.dot(a, b, preferred_element_type=F32)


def _const_spec(*shape):
    return pl.BlockSpec(shape, lambda *_: (0,) * len(shape), pipeline_mode=pl.Buffered(1))


def _mixer_body(sinks_ref, x_ref, g_ref, w_in_ref, conv_w_ref, pool_w_ref, pool_scale_ref,
                sgu_w_ref, sgu_b_ref, w_br_ref, w_out_ref, o_ref,
                kv_buf, z_buf, p_buf, y_buf, merged_buf):
    tm = x_ref.shape[0]
    nblk = tm // ATT_BLOCK
    i = pl.program_id(0)

    @pl.when(i == 0)
    def _():
        kv_buf[pl.ds(0, ATT_BLOCK), :] = jnp.zeros((ATT_BLOCK, kv_buf.shape[1]), F32)
        z_buf[pl.ds(0, CONV_HALO), :] = jnp.zeros((CONV_HALO, z_buf.shape[1]), F32)
        p_buf[pl.ds(0, POOL_HALO), :] = jnp.zeros((POOL_HALO, p_buf.shape[1]), F32)

    x = x_ref[...]
    hb = _rmsnorm(x, g_ref[...]).astype(BF16)

    def proj(c0, n):
        return _dot(hb, w_in_ref[:, c0:c0 + n])

    def gated_up(n):
        gate = _sigmoid(proj(_C_GATE + n * D_MODEL, D_MODEL))
        return gate * _dot(y_buf[...], w_br_ref[n])

    qkv = proj(_C_Q, 768)
    kv_buf[pl.ds(ATT_BLOCK, tm), :] = qkv[:, 512:768]
    q = qkv[:, 0:512] * (HEAD_DIM ** -0.5)
    lo128 = lax.broadcasted_iota(jnp.int32, (ATT_BLOCK, LANES), 1) < HEAD_DIM
    lo256 = lax.broadcasted_iota(jnp.int32, (2 * ATT_BLOCK, LANES), 1) < HEAD_DIM
    qi = lax.broadcasted_iota(jnp.int32, (ATT_BLOCK, 2 * ATT_BLOCK), 0)
    kj = lax.broadcasted_iota(jnp.int32, (ATT_BLOCK, 2 * ATT_BLOCK), 1)
    rel = qi + ATT_BLOCK - kj
    local = (rel >= 0) & (rel < ATT_BLOCK)
    for b in range(nblk):
        kvb = kv_buf[pl.ds(b * ATT_BLOCK, 2 * ATT_BLOCK), :]
        kb, vb = kvb[:, 0:LANES], kvb[:, LANES:2 * LANES]
        kr, vr = pltpu.roll(kb, HEAD_DIM, 1), pltpu.roll(vb, HEAD_DIM, 1)
        kkT = [jnp.where(lo256, kb, kr).T.astype(BF16), jnp.where(lo256, kr, kb).T.astype(BF16)]
        vv = [jnp.where(lo256, vb, vr).astype(BF16), jnp.where(lo256, vr, vb).astype(BF16)]
        valid = local & (kj + (i * tm + (b - 1) * ATT_BLOCK) >= 0)
        for j in range(ATT_HEADS // 2):
            qp = q[b * ATT_BLOCK:(b + 1) * ATT_BLOCK, j * LANES:(j + 1) * LANES]
            kvh = (2 * j) // (ATT_HEADS // ATT_KV_HEADS)
            outs = []
            for half in range(2):
                keep = lo128 if half == 0 else jnp.logical_not(lo128)
                qm = jnp.where(keep, qp, 0.0).astype(BF16)
                s = jnp.where(valid, _dot(qm, kkT[kvh]), NEG_INF)
                sink = sinks_ref[2 * j + half]
                m = jnp.maximum(jnp.max(s, axis=-1, keepdims=True), sink)
                p = jnp.exp(s - m)
                den = jnp.sum(p, axis=-1, keepdims=True) + jnp.exp(sink - m)
                outs.append(_dot(p.astype(BF16), vv[kvh]) / den)
            y_buf[pl.ds(b * ATT_BLOCK, ATT_BLOCK), j * LANES:(j + 1) * LANES] = (
                jnp.where(lo128, outs[0], outs[1]).astype(BF16))
    kv_buf[pl.ds(0, ATT_BLOCK), :] = kv_buf[pl.ds(tm, ATT_BLOCK), :]
    merged_buf[...] = gated_up(0)

    c3 = proj(_C_CONV, 3 * BRANCH_WIDTH)
    z = c3[:, 512:1024] * c3[:, 1024:1536]
    z_buf[pl.ds(CONV_HALO, tm), :] = z
    cw = conv_w_ref[...]
    conv = (cw[0:1] * z_buf[pl.ds(CONV_HALO - 2, tm), :] + cw[1:2] * z_buf[pl.ds(CONV_HALO - 1, tm), :]
            + cw[2:3] * z)
    z_buf[pl.ds(0, CONV_HALO), :] = z_buf[pl.ds(tm, CONV_HALO), :]
    y_buf[...] = (c3[:, 0:512] * conv).astype(BF16)
    merged_buf[...] += gated_up(1)

    pp = proj(_C_POOL, BRANCH_WIDTH)
    p_buf[pl.ds(POOL_HALO, tm), :] = pp
    tpos = i * tm + lax.broadcasted_iota(jnp.int32, (tm, GROUP), 0)
    for gi, w in enumerate(POOL_SIZES):
        cs = slice(gi * GROUP, (gi + 1) * GROUP)
        win = pp[:, cs]
        for k in range(1, w):
            win = win + p_buf[pl.ds(POOL_HALO - k, tm), cs]
        cnt = jnp.minimum(tpos + 1, w).astype(F32)
        pooled = win / cnt - pp[:, cs]
        yg = _dot(pooled.astype(BF16), pool_w_ref[gi]) * pool_scale_ref[:, cs]
        y_buf[:, cs] = yg.astype(BF16)
    p_buf[pl.ds(0, POOL_HALO), :] = p_buf[pl.ds(tm, POOL_HALO), :]
    merged_buf[...] += gated_up(2)

    usv = proj(_C_SGU, 2 * BRANCH_WIDTH)
    tril = (lax.broadcasted_iota(jnp.int32, (GROUP, GROUP), 0)
            >= lax.broadcasted_iota(jnp.int32, (GROUP, GROUP), 1))
    for g in range(BRANCH_WIDTH // GROUP):
        wg = jnp.where(tril, sgu_w_ref[g], jnp.zeros((GROUP, GROUP), BF16))
        for c in range(nblk):
            rows = slice(c * GROUP, (c + 1) * GROUP)
            sv = usv[rows, BRANCH_WIDTH + g * GROUP:BRANCH_WIDTH + (g + 1) * GROUP].astype(BF16)
            mixed = _dot(wg, sv) + sgu_b_ref[g]
            y_buf[pl.ds(c * GROUP, GROUP), g * GROUP:(g + 1) * GROUP] = (
                usv[rows, g * GROUP:(g + 1) * GROUP] * mixed).astype(BF16)
    merged_buf[...] += gated_up(3)

    o_ref[...] = x + _dot(merged_buf[...].astype(BF16), w_out_ref[...])


def _mixer(x, ln_g, w_in, sinks, conv_w, pool_w, pool_scale, sgu_w, sgu_b_full, w_br, w_out):
    s = x.shape[0]
    tm = MIX_TM
    return pl.pallas_call(
        _mixer_body,
        grid=(s // tm,),
        in_specs=[
            pl.BlockSpec(memory_space=pltpu.SMEM),
            pl.BlockSpec((tm, D_MODEL), lambda i: (i, 0)),
            _const_spec(1, D_MODEL),
            _const_spec(D_MODEL, IN_COLS),
            _const_spec(3, BRANCH_WIDTH),
            _const_spec(4, GROUP, GROUP),
            _const_spec(1, BRANCH_WIDTH),
            _const_spec(4, GROUP, GROUP),
            _const_spec(4, GROUP, GROUP),
            _const_spec(N_BRANCH, BRANCH_WIDTH, D_MODEL),
            _const_spec(D_MODEL, D_MODEL),
        ],
        out_specs=pl.BlockSpec((tm, D_MODEL), lambda i: (i, 0)),
        out_shape=jax.ShapeDtypeStruct((s, D_MODEL), F32),
        scratch_shapes=[
            pltpu.VMEM((tm + ATT_BLOCK, 2 * LANES), F32),
            pltpu.VMEM((tm + CONV_HALO, BRANCH_WIDTH), F32),
            pltpu.VMEM((tm + POOL_HALO, BRANCH_WIDTH), F32),
            pltpu.VMEM((tm, BRANCH_WIDTH), BF16),
            pltpu.VMEM((tm, D_MODEL), F32),
        ],
        compiler_params=pltpu.CompilerParams(
            dimension_semantics=("arbitrary",), vmem_limit_bytes=VMEM_LIMIT_BYTES),
        name="mixer",
    )(sinks, x, ln_g, w_in, conv_w, pool_w, pool_scale, sgu_w, sgu_b_full, w_br, w_out)


def _swiglu_into(hb, wg_ref, wu_ref, wd_ref, acc_ref, base):
    for c in range(FFN_HIDDEN // FFN_FC):
        cs = slice(c * FFN_FC, (c + 1) * FFN_FC)
        g = _dot(hb, wg_ref[:, cs])
        u = _dot(hb, wu_ref[:, cs])
        a = (g * _sigmoid(g) * u).astype(BF16)
        d = _dot(a, wd_ref[cs, :])
        if c == 0:
            acc_ref[...] = d if base is None else base + d
        else:
            acc_ref[...] += d


def _dense_ffn_body(x_ref, g_ref, wg_ref, wu_ref, wd_ref, o_ref):
    x = x_ref[...]
    hb = _rmsnorm(x, g_ref[...]).astype(BF16)
    _swiglu_into(hb, wg_ref, wu_ref, wd_ref, o_ref, x)


def _dense_ffn(x, ln_g, wg, wu, wd):
    s = x.shape[0]
    tm = FFN_TM
    return pl.pallas_call(
        _dense_ffn_body,
        grid=(s // tm,),
        in_specs=[
            pl.BlockSpec((tm, D_MODEL), lambda i: (i, 0)),
            _const_spec(1, D_MODEL),
            _const_spec(D_MODEL, FFN_HIDDEN),
            _const_spec(D_MODEL, FFN_HIDDEN),
            _const_spec(FFN_HIDDEN, D_MODEL),
        ],
        out_specs=pl.BlockSpec((tm, D_MODEL), lambda i: (i, 0)),
        out_shape=jax.ShapeDtypeStruct((s, D_MODEL), F32),
        compiler_params=pltpu.CompilerParams(
            dimension_semantics=("arbitrary",), vmem_limit_bytes=VMEM_LIMIT_BYTES),
        name="dense_ffn",
    )(x, ln_g, wg, wu, wd)


def _expert_ffn_body(tile_expert_ref, n_used_ref, hs_ref, wg_ref, wu_ref, wd_ref, o_ref):
    del tile_expert_ref
    i = pl.program_id(0)

    @pl.when(i < n_used_ref[0])
    def _():
        _swiglu_into(hs_ref[...].astype(BF16), wg_ref, wu_ref, wd_ref, o_ref, None)

    @pl.when(i >= n_used_ref[0])
    def _():
        o_ref[...] = jnp.zeros(o_ref.shape, F32)


def _expert_ffn(tile_expert, n_used, hs, wg, wu, wd):
    rows = hs.shape[0]
    tm = FFN_TM
    w_in_spec = pl.BlockSpec((None, D_MODEL, FFN_HIDDEN), lambda i, te, nu: (te[i], 0, 0))
    w_out_spec = pl.BlockSpec((None, FFN_HIDDEN, D_MODEL), lambda i, te, nu: (te[i], 0, 0))
    return pl.pallas_call(
        _expert_ffn_body,
        grid_spec=pltpu.PrefetchScalarGridSpec(
            num_scalar_prefetch=2,
            grid=(rows // tm,),
            in_specs=[
                pl.BlockSpec((tm, D_MODEL), lambda i, te, nu: (jnp.minimum(i, nu[0] - 1), 0)),
                w_in_spec, w_in_spec, w_out_spec,
            ],
            out_specs=pl.BlockSpec((tm, D_MODEL), lambda i, te, nu: (i, 0)),
        ),
        out_shape=jax.ShapeDtypeStruct((rows, D_MODEL), F32),
        compiler_params=pltpu.CompilerParams(
            dimension_semantics=("arbitrary",), vmem_limit_bytes=VMEM_LIMIT_BYTES),
        name="expert_ffn",
    )(tile_expert, n_used, hs, wg, wu, wd)


def _router_body(x_ref, g_ref, rw_ref, rb_ref, h_ref, meta_ref, cnt_ref, carry):
    tm = x_ref.shape[0]
    i = pl.program_id(0)

    @pl.when(i == 0)
    def _():
        carry[...] = jnp.zeros(carry.shape, F32)

    h = _rmsnorm(x_ref[...], g_ref[...])
    h_ref[...] = h
    h_hi = h.astype(BF16)
    h_lo = (h - h_hi.astype(F32)).astype(BF16)
    w = rw_ref[...]
    w_hi = w.astype(BF16)
    w_lo = (w - w_hi.astype(F32)).astype(BF16)
    logits = _dot(h_hi, w_hi) + (_dot(h_lo, w_hi) + _dot(h_hi, w_lo)) + rb_ref[...]
    lane = lax.broadcasted_iota(jnp.int32, (tm, LANES), 1)
    logits = jnp.where(lane < N_EXPERTS, logits, NEG_INF)
    m1 = jnp.max(logits, axis=-1, keepdims=True)
    i1 = jnp.min(jnp.where(logits == m1, lane, LANES), axis=-1, keepdims=True)
    rest = jnp.where(lane == i1, NEG_INF, logits)
    m2 = jnp.max(rest, axis=-1, keepdims=True)
    i2 = jnp.min(jnp.where(rest == m2, lane, LANES), axis=-1, keepdims=True)
    e = jnp.exp(m2 - m1)
    w1 = 1.0 / (1.0 + e)
    w2 = e / (1.0 + e)
    oh1, oh2 = lane == i1, lane == i2
    cnt = jnp.where(oh1 | oh2, 1.0, 0.0)
    below = (lax.broadcasted_iota(jnp.int32, (tm, tm), 0) > lax.broadcasted_iota(jnp.int32, (tm, tm), 1))
    prefix = _dot(jnp.where(below, 1.0, 0.0).astype(BF16), cnt.astype(BF16)) + carry[0:1, :]
    r1 = jnp.sum(jnp.where(oh1, prefix, 0.0), axis=-1, keepdims=True)
    r2 = jnp.sum(jnp.where(oh2, prefix, 0.0), axis=-1, keepdims=True)
    meta = jnp.zeros((tm, LANES), F32)
    for k, v in enumerate((i1.astype(F32), i2.astype(F32), w1, w2, r1, r2)):
        meta = jnp.where(lane == k, v, meta)
    meta_ref[...] = meta
    carry[...] += jnp.sum(cnt, axis=0, keepdims=True)
    cnt_ref[...] = carry[...]


def _router(x, ln_g, rw_pad, rb_pad):
    s = x.shape[0]
    tm = ROUTER_TM
    return pl.pallas_call(
        _router_body,
        grid=(s // tm,),
        in_specs=[
            pl.BlockSpec((tm, D_MODEL), lambda i: (i, 0)),
            _const_spec(1, D_MODEL),
            _const_spec(D_MODEL, LANES),
            _const_spec(1, LANES),
        ],
        out_specs=[
            pl.BlockSpec((tm, D_MODEL), lambda i: (i, 0)),
            pl.BlockSpec((tm, LANES), lambda i: (i, 0)),
            pl.BlockSpec((SUBLANES, LANES), lambda i: (0, 0)),
        ],
        out_shape=[
            jax.ShapeDtypeStruct((s, D_MODEL), F32),
            jax.ShapeDtypeStruct((s, LANES), F32),
            jax.ShapeDtypeStruct((SUBLANES, LANES), F32),
        ],
        scratch_shapes=[pltpu.VMEM((SUBLANES, LANES), F32)],
        compiler_params=pltpu.CompilerParams(
            dimension_semantics=("arbitrary",), vmem_limit_bytes=VMEM_LIMIT_BYTES),
        name="router",
    )(x, ln_g, rw_pad, rb_pad)


def _row_copy(src_ref, src_row, dst_ref, dst_row, sem):
    return pltpu.make_async_copy(src_ref.at[pl.ds(src_row, 1)], dst_ref.at[pl.ds(dst_row, 1)], sem)


def _gather_body(dest_ref, h_ref, hs_init_ref, hs_ref, sem):
    del hs_init_ref
    i = pl.program_id(0)

    def start(r, c):
        t = i * GATHER_T + r
        _row_copy(h_ref, r, hs_ref, dest_ref[2 * t], sem).start()
        _row_copy(h_ref, r, hs_ref, dest_ref[2 * t + 1], sem).start()
        return c

    lax.fori_loop(0, GATHER_T, start, 0, unroll=8)

    def wait(r, c):
        _row_copy(h_ref, 0, hs_ref, 0, sem).wait()
        return c

    lax.fori_loop(0, 2 * GATHER_T, wait, 0, unroll=8)


def _gather_rows(dest, h, rows):
    s = h.shape[0]
    return pl.pallas_call(
        _gather_body,
        grid_spec=pltpu.PrefetchScalarGridSpec(
            num_scalar_prefetch=1,
            grid=(s // GATHER_T,),
            in_specs=[pl.BlockSpec((GATHER_T, D_MODEL), lambda i, d: (i, 0)),
                      pl.BlockSpec(memory_space=pl.ANY)],
            out_specs=pl.BlockSpec(memory_space=pl.ANY),
            scratch_shapes=[pltpu.SemaphoreType.DMA],
        ),
        out_shape=jax.ShapeDtypeStruct((rows, D_MODEL), F32),
        input_output_aliases={2: 0},
        compiler_params=pltpu.CompilerParams(dimension_semantics=("arbitrary",)),
        name="gather_rows",
    )(dest, h, jnp.zeros((rows, D_MODEL), F32))


def _combine_body(dest_ref, x_ref, meta_ref, ys_ref, *rest, final_norm):
    if final_norm:
        g_ref, o_ref, y_buf, sem = rest
    else:
        o_ref, y_buf, sem = rest
    i = pl.program_id(0)

    def start(r, c):
        t = i * GATHER_T + r
        _row_copy(ys_ref, dest_ref[2 * t], y_buf.at[0], r, sem).start()
        _row_copy(ys_ref, dest_ref[2 * t + 1], y_buf.at[1], r, sem).start()
        return c

    lax.fori_loop(0, GATHER_T, start, 0, unroll=8)

    def wait(r, c):
        _row_copy(ys_ref, 0, y_buf.at[0], 0, sem).wait()
        return c

    lax.fori_loop(0, 2 * GATHER_T, wait, 0, unroll=8)

    meta = meta_ref[...]
    lane = lax.broadcasted_iota(jnp.int32, meta.shape, 1)
    w1 = jnp.sum(jnp.where(lane == 2, meta, 0.0), axis=-1, keepdims=True)
    w2 = jnp.sum(jnp.where(lane == 3, meta, 0.0), axis=-1, keepdims=True)
    y = x_ref[...] + (w1 * y_buf[0] + w2 * y_buf[1])
    o_ref[...] = _rmsnorm(y, g_ref[...]) if final_norm else y


def _combine(dest, x, meta, ys, final_g=None):
    s = x.shape[0]
    t = GATHER_T
    in_specs = [
        pl.BlockSpec((t, D_MODEL), lambda i, d: (i, 0)),
        pl.BlockSpec((t, LANES), lambda i, d: (i, 0)),
        pl.BlockSpec(memory_space=pl.ANY),
    ]
    args = [dest, x, meta, ys]
    if final_g is not None:
        in_specs.append(pl.BlockSpec((1, D_MODEL), lambda i, d: (0, 0)))
        args.append(final_g)
    return pl.pallas_call(
        functools.partial(_combine_body, final_norm=final_g is not None),
        grid_spec=pltpu.PrefetchScalarGridSpec(
            num_scalar_prefetch=1,
            grid=(s // t,),
            in_specs=in_specs,
            out_specs=pl.BlockSpec((t, D_MODEL), lambda i, d: (i, 0)),
            scratch_shapes=[pltpu.VMEM((2, t, D_MODEL), F32), pltpu.SemaphoreType.DMA],
        ),
        out_shape=jax.ShapeDtypeStruct((s, D_MODEL), F32),
        compiler_params=pltpu.CompilerParams(dimension_semantics=("arbitrary",)),
        name="combine",
    )(*args)


def _moe_ffn(x, ln_g, router_w, router_b, wg, wu, wd, final_g):
    s = x.shape[0]
    tm = FFN_TM
    rw_pad = jnp.zeros((D_MODEL, LANES), F32).at[:, :N_EXPERTS].set(router_w)
    rb_pad = jnp.zeros((1, LANES), F32).at[0, :N_EXPERTS].set(router_b)
    h, meta, counts = _router(x, ln_g, rw_pad, rb_pad)
    idx = meta[:, 0:2].astype(jnp.int32)
    rank = meta[:, 4:6].astype(jnp.int32)
    cnt = counts[0, :N_EXPERTS].astype(jnp.int32)
    padded = (cnt + tm - 1) // tm * tm
    ends = jnp.cumsum(padded)
    dest = ((ends - padded)[idx] + rank).reshape(-1)
    n_tiles = 2 * s // tm + N_EXPERTS
    tile_start = jnp.arange(n_tiles, dtype=jnp.int32) * tm
    tile_expert = jnp.minimum(
        jnp.sum(tile_start[:, None] >= ends[None, :], axis=1), N_EXPERTS - 1).astype(jnp.int32)
    n_used = (ends[-1:] // tm).astype(jnp.int32)
    hs = _gather_rows(dest, h, n_tiles * tm)
    ys = _expert_ffn(tile_expert, n_used, hs, wg, wu, wd)
    return _combine(dest, x, meta, ys, final_g)


def kernel(x, ln_mix_g, w_in, attn_sinks, conv_w, pool_w, pool_scale, sgu_w, sgu_b, w_branch, w_out,
           ln_ffn_g, dense_w_gate, dense_w_up, dense_w_down, router_w, router_b,
           moe_w_gate, moe_w_up, moe_w_down, ln_final_g):
    batch, seq, _ = x.shape
    depth = w_in.shape[0]
    assert batch == 1 and seq % MIX_TM == 0 and seq % FFN_TM == 0 and depth % 2 == 0
    xs = x.reshape(seq, D_MODEL)
    for layer in range(depth):
        sgu_b_full = jnp.broadcast_to(sgu_b[layer][:, :, None], (4, GROUP, GROUP))
        xs = _mixer(xs, ln_mix_g[layer][None], w_in[layer].astype(BF16), attn_sinks[layer],
                    conv_w[layer], pool_w[layer].astype(BF16), pool_scale[layer][None],
                    sgu_w[layer].astype(BF16), sgu_b_full, w_branch[layer].astype(BF16),
                    w_out[layer].astype(BF16))
        i = layer // 2
        if layer % 2 == 0:
            xs = _dense_ffn(xs, ln_ffn_g[layer][None], dense_w_gate[i].astype(BF16),
                            dense_w_up[i].astype(BF16), dense_w_down[i].astype(BF16))
        else:
            final_g = ln_final_g[None] if layer == depth - 1 else None
            xs = _moe_ffn(xs, ln_ffn_g[layer][None], router_w[i], router_b[i],
                          moe_w_gate[i].astype(BF16), moe_w_up[i].astype(BF16),
                          moe_w_down[i].astype(BF16), final_g)
    return xs.reshape(batch, seq, D_MODEL)
```

```python
import functools

import jax
import jax.numpy as jnp
from jax import lax
from jax.experimental import pallas as pl
from jax.experimental.pallas import tpu as pltpu

F32 = jnp.float32
BF16 = jnp.bfloat16

D_MODEL = 1024
ATT_HEADS = 8
ATT_KV_HEADS = 2
HEAD_DIM = 64
ATT_BLOCK = 128
POOL_SIZES = (2, 4, 8, 16)
GROUP = 128
N_BRANCH = 4
BRANCH_WIDTH = 512
FFN_HIDDEN = 2816
N_EXPERTS = 8
RMS_EPS = 1e-6
NEG_INF = -1e30

_C_Q, _C_KV, _C_CONV, _C_POOL, _C_SGU, _C_GATE = 0, 512, 768, 2304, 2816, 3840
IN_COLS = _C_GATE + N_BRANCH * D_MODEL

LANES = 128
SUBLANES = 8
TOKEN_TILE = (SUBLANES, LANES)
MXU_COLS = 256
VMEM_LIMIT_BYTES = 60 * 1024 * 1024

MIX_TM = 512
FFN_TM = 512
FFN_FC = MXU_COLS
ROUTER_TM = 512
GATHER_T = 256
CONV_HALO = SUBLANES
POOL_HALO = 2 * SUBLANES


def _rmsnorm(x, g):
    return x * lax.rsqrt(jnp.mean(x * x, axis=-1, keepdims=True) + RMS_EPS) * g


def _sigmoid(x):
    return 1.0 / (1.0 + jnp.exp(-x))


def _dot(a, b):
    return jnp.dot(a, b, preferred_element_type=F32)


def _const_spec(*shape):
    return pl.BlockSpec(shape, lambda *_: (0,) * len(shape), pipeline_mode=pl.Buffered(1))


def _mixer_body(sinks_ref, x_ref, g_ref, w_in_ref, conv_w_ref, pool_w_ref, pool_scale_ref,
                sgu_w_ref, sgu_b_ref, w_br_ref, w_out_ref, o_ref,
                kv_buf, z_buf, p_buf, y_buf, gate_buf):
    tm = x_ref.shape[0]
    nblk = tm // ATT_BLOCK
    i = pl.program_id(0)

    @pl.when(i == 0)
    def _():
        kv_buf[pl.ds(0, ATT_BLOCK), :] = jnp.zeros((ATT_BLOCK, kv_buf.shape[1]), F32)
        z_buf[pl.ds(0, CONV_HALO), :] = jnp.zeros((CONV_HALO, z_buf.shape[1]), F32)
        p_buf[pl.ds(0, POOL_HALO), :] = jnp.zeros((POOL_HALO, p_buf.shape[1]), F32)

    x = x_ref[...]
    hb = _rmsnorm(x, g_ref[...]).astype(BF16)

    def proj(c0, n):
        return _dot(hb, w_in_ref[:, c0:c0 + n])

    gate_chunks = [(n, c) for n in range(N_BRANCH) for c in range(D_MODEL // MXU_COLS)]

    def gate_chunk():
        n, c = gate_chunks.pop(0)
        cols = slice(c * MXU_COLS, (c + 1) * MXU_COLS)
        gate_buf[n, :, cols] = _sigmoid(proj(_C_GATE + n * D_MODEL + c * MXU_COLS, MXU_COLS))

    qkv = proj(_C_Q, 768)
    kv_buf[pl.ds(ATT_BLOCK, tm), :] = qkv[:, 512:768]
    q = qkv[:, 0:512] * (HEAD_DIM ** -0.5)
    c3 = proj(_C_CONV, 3 * BRANCH_WIDTH)
    pp = proj(_C_POOL, BRANCH_WIDTH)
    usv = proj(_C_SGU, 2 * BRANCH_WIDTH)

    z = c3[:, 512:1024] * c3[:, 1024:1536]
    z_buf[pl.ds(CONV_HALO, tm), :] = z
    cw = conv_w_ref[...]
    conv = (cw[0:1] * z_buf[pl.ds(CONV_HALO - 2, tm), :] + cw[1:2] * z_buf[pl.ds(CONV_HALO - 1, tm), :]
            + cw[2:3] * z)
    z_buf[pl.ds(0, CONV_HALO), :] = z_buf[pl.ds(tm, CONV_HALO), :]
    y_buf[1] = (c3[:, 0:512] * conv).astype(BF16)

    p_buf[pl.ds(POOL_HALO, tm), :] = pp
    tpos = i * tm + lax.broadcasted_iota(jnp.int32, (tm, GROUP), 0)
    pooled = []
    for gi, w in enumerate(POOL_SIZES):
        cs = slice(gi * GROUP, (gi + 1) * GROUP)
        win = pp[:, cs]
        for k in range(1, w):
            win = win + p_buf[pl.ds(POOL_HALO - k, tm), cs]
        cnt = jnp.minimum(tpos + 1, w).astype(F32)
        pooled.append((win / cnt - pp[:, cs]).astype(BF16))
    p_buf[pl.ds(0, POOL_HALO), :] = p_buf[pl.ds(tm, POOL_HALO), :]

    def pool_linear():
        for gi in range(len(POOL_SIZES)):
            cs = slice(gi * GROUP, (gi + 1) * GROUP)
            y_buf[2, :, cs] = (_dot(pooled[gi], pool_w_ref[gi]) * pool_scale_ref[:, cs]).astype(BF16)

    def sgu_mix():
        tril = (lax.broadcasted_iota(jnp.int32, (GROUP, GROUP), 0)
                >= lax.broadcasted_iota(jnp.int32, (GROUP, GROUP), 1))
        for g in range(BRANCH_WIDTH // GROUP):
            wg = jnp.where(tril, sgu_w_ref[g], jnp.zeros((GROUP, GROUP), BF16))
            for c in range(nblk):
                rows = slice(c * GROUP, (c + 1) * GROUP)
                sv = usv[rows, BRANCH_WIDTH + g * GROUP:BRANCH_WIDTH + (g + 1) * GROUP].astype(BF16)
                mixed = _dot(wg, sv) + sgu_b_ref[g]
                y_buf[3, pl.ds(c * GROUP, GROUP), g * GROUP:(g + 1) * GROUP] = (
                    usv[rows, g * GROUP:(g + 1) * GROUP] * mixed).astype(BF16)

    lo128 = lax.broadcasted_iota(jnp.int32, (ATT_BLOCK, LANES), 1) < HEAD_DIM
    lo256 = lax.broadcasted_iota(jnp.int32, (2 * ATT_BLOCK, LANES), 1) < HEAD_DIM
    qi = lax.broadcasted_iota(jnp.int32, (ATT_BLOCK, 2 * ATT_BLOCK), 0)
    kj = lax.broadcasted_iota(jnp.int32, (ATT_BLOCK, 2 * ATT_BLOCK), 1)
    rel = qi + ATT_BLOCK - kj
    local = (rel >= 0) & (rel < ATT_BLOCK)

    def attention_operands(b):
        kvb = kv_buf[pl.ds(b * ATT_BLOCK, 2 * ATT_BLOCK), :]
        kb, vb = kvb[:, 0:LANES], kvb[:, LANES:2 * LANES]
        kr, vr = pltpu.roll(kb, HEAD_DIM, 1), pltpu.roll(vb, HEAD_DIM, 1)
        kkT = [jnp.where(lo256, kb, kr).T.astype(BF16), jnp.where(lo256, kr, kb).T.astype(BF16)]
        vv = [jnp.where(lo256, vb, vr).astype(BF16), jnp.where(lo256, vr, vb).astype(BF16)]
        valid = local & (kj + (i * tm + (b - 1) * ATT_BLOCK) >= 0)
        return kkT, vv, valid

    operands = attention_operands(0)
    for b in range(nblk):
        kkT, vv, valid = operands
        for j in range(ATT_HEADS // 2):
            qp = q[b * ATT_BLOCK:(b + 1) * ATT_BLOCK, j * LANES:(j + 1) * LANES]
            kvh = (2 * j) // (ATT_HEADS // ATT_KV_HEADS)
            scores = []
            for half in range(2):
                keep = lo128 if half == 0 else jnp.logical_not(lo128)
                qm = jnp.where(keep, qp, 0.0).astype(BF16)
                scores.append(jnp.where(valid, _dot(qm, kkT[kvh]), NEG_INF))
            if j == 1 and b + 1 < nblk:
                operands = attention_operands(b + 1)
            gate_chunk()
            outs = []
            for half in range(2):
                s = scores[half]
                sink = sinks_ref[2 * j + half]
                m = jnp.maximum(jnp.max(s, axis=-1, keepdims=True), sink)
                p = jnp.exp(s - m)
                den = jnp.sum(p, axis=-1, keepdims=True) + jnp.exp(sink - m)
                outs.append(_dot(p.astype(BF16), vv[kvh]) / den)
            y_buf[0, pl.ds(b * ATT_BLOCK, ATT_BLOCK), j * LANES:(j + 1) * LANES] = (
                jnp.where(lo128, outs[0], outs[1]).astype(BF16))
        if b == 0:
            pool_linear()
        if b == 1:
            sgu_mix()
    kv_buf[pl.ds(0, ATT_BLOCK), :] = kv_buf[pl.ds(tm, ATT_BLOCK), :]
    while gate_chunks:
        gate_chunk()

    merged = gate_buf[0] * _dot(y_buf[0], w_br_ref[0])
    for n in range(1, N_BRANCH):
        merged = merged + gate_buf[n] * _dot(y_buf[n], w_br_ref[n])
    o_ref[...] = x + _dot(merged.astype(BF16), w_out_ref[...])


def _mixer(x, ln_g, w_in, sinks, conv_w, pool_w, pool_scale, sgu_w, sgu_b_full, w_br, w_out):
    s = x.shape[0]
    tm = MIX_TM
    return pl.pallas_call(
        _mixer_body,
        grid=(s // tm,),
        in_specs=[
            pl.BlockSpec(memory_space=pltpu.SMEM),
            pl.BlockSpec((tm, D_MODEL), lambda i: (i, 0)),
            _const_spec(1, D_MODEL),
            _const_spec(D_MODEL, IN_COLS),
            _const_spec(3, BRANCH_WIDTH),
            _const_spec(4, GROUP, GROUP),
            _const_spec(1, BRANCH_WIDTH),
            _const_spec(4, GROUP, GROUP),
            _const_spec(4, GROUP, GROUP),
            _const_spec(N_BRANCH, BRANCH_WIDTH, D_MODEL),
            _const_spec(D_MODEL, D_MODEL),
        ],
        out_specs=pl.BlockSpec((tm, D_MODEL), lambda i: (i, 0)),
        out_shape=jax.ShapeDtypeStruct((s, D_MODEL), F32),
        scratch_shapes=[
            pltpu.VMEM((tm + ATT_BLOCK, 2 * LANES), F32),
            pltpu.VMEM((tm + CONV_HALO, BRANCH_WIDTH), F32),
            pltpu.VMEM((tm + POOL_HALO, BRANCH_WIDTH), F32),
            pltpu.VMEM((N_BRANCH, tm, BRANCH_WIDTH), BF16),
            pltpu.VMEM((N_BRANCH, tm, D_MODEL), F32),
        ],
        compiler_params=pltpu.CompilerParams(
            dimension_semantics=("arbitrary",), vmem_limit_bytes=VMEM_LIMIT_BYTES),
        name="mixer",
    )(sinks, x, ln_g, w_in, conv_w, pool_w, pool_scale, sgu_w, sgu_b_full, w_br, w_out)


def _swiglu_into(hb, wg_ref, wu_ref, wd_ref, acc_ref, base):
    for c in range(FFN_HIDDEN // FFN_FC):
        cs = slice(c * FFN_FC, (c + 1) * FFN_FC)
        g = _dot(hb, wg_ref[:, cs])
        u = _dot(hb, wu_ref[:, cs])
        a = (g * _sigmoid(g) * u).astype(BF16)
        d = _dot(a, wd_ref[cs, :])
        if c == 0:
            acc_ref[...] = d if base is None else base + d
        else:
            acc_ref[...] += d


def _dense_ffn_body(x_ref, g_ref, wg_ref, wu_ref, wd_ref, o_ref):
    x = x_ref[...]
    hb = _rmsnorm(x, g_ref[...]).astype(BF16)
    _swiglu_into(hb, wg_ref, wu_ref, wd_ref, o_ref, x)


def _dense_ffn(x, ln_g, wg, wu, wd):
    s = x.shape[0]
    tm = FFN_TM
    return pl.pallas_call(
        _dense_ffn_body,
        grid=(s // tm,),
        in_specs=[
            pl.BlockSpec((tm, D_MODEL), lambda i: (i, 0)),
            _const_spec(1, D_MODEL),
            _const_spec(D_MODEL, FFN_HIDDEN),
            _const_spec(D_MODEL, FFN_HIDDEN),
            _const_spec(FFN_HIDDEN, D_MODEL),
        ],
        out_specs=pl.BlockSpec((tm, D_MODEL), lambda i: (i, 0)),
        out_shape=jax.ShapeDtypeStruct((s, D_MODEL), F32),
        compiler_params=pltpu.CompilerParams(
            dimension_semantics=("arbitrary",), vmem_limit_bytes=VMEM_LIMIT_BYTES),
        name="dense_ffn",
    )(x, ln_g, wg, wu, wd)


def _expert_ffn_body(tile_expert_ref, n_used_ref, hs_ref, wg_ref, wu_ref, wd_ref, o_ref, acc_ref):
    del tile_expert_ref
    i = pl.program_id(0)
    tm = hs_ref.shape[0]

    @pl.when(i < n_used_ref[0])
    def _():
        hb = hs_ref[...].reshape(tm, D_MODEL).astype(BF16)
        _swiglu_into(hb, wg_ref, wu_ref, wd_ref, acc_ref, None)
        o_ref[...] = acc_ref[...].reshape(o_ref.shape)

    @pl.when(i >= n_used_ref[0])
    def _():
        o_ref[...] = jnp.zeros(o_ref.shape, F32)


def _expert_ffn(tile_expert, n_used, hs, wg, wu, wd):
    rows = hs.shape[0]
    tm = FFN_TM
    w_in_spec = pl.BlockSpec((None, D_MODEL, FFN_HIDDEN), lambda i, te, nu: (te[i], 0, 0))
    w_out_spec = pl.BlockSpec((None, FFN_HIDDEN, D_MODEL), lambda i, te, nu: (te[i], 0, 0))
    return pl.pallas_call(
        _expert_ffn_body,
        grid_spec=pltpu.PrefetchScalarGridSpec(
            num_scalar_prefetch=2,
            grid=(rows // tm,),
            in_specs=[
                pl.BlockSpec((tm,) + TOKEN_TILE, lambda i, te, nu: (jnp.minimum(i, nu[0] - 1), 0, 0)),
                w_in_spec, w_in_spec, w_out_spec,
            ],
            out_specs=pl.BlockSpec((tm,) + TOKEN_TILE, lambda i, te, nu: (i, 0, 0)),
            scratch_shapes=[pltpu.VMEM((tm, D_MODEL), F32)],
        ),
        out_shape=jax.ShapeDtypeStruct((rows,) + TOKEN_TILE, F32),
        compiler_params=pltpu.CompilerParams(
            dimension_semantics=("arbitrary",), vmem_limit_bytes=VMEM_LIMIT_BYTES),
        name="expert_ffn",
    )(tile_expert, n_used, hs, wg, wu, wd)


def _router_body(x_ref, g_ref, rw_ref, rb_ref, h_ref, meta_ref, cnt_ref, carry):
    tm = x_ref.shape[0]
    i = pl.program_id(0)

    @pl.when(i == 0)
    def _():
        carry[...] = jnp.zeros(carry.shape, F32)

    h = _rmsnorm(x_ref[...], g_ref[...])
    h_ref[...] = h.reshape(h_ref.shape)
    h_hi = h.astype(BF16)
    h_lo = (h - h_hi.astype(F32)).astype(BF16)
    w = rw_ref[...]
    w_hi = w.astype(BF16)
    w_lo = (w - w_hi.astype(F32)).astype(BF16)
    logits = _dot(h_hi, w_hi) + (_dot(h_lo, w_hi) + _dot(h_hi, w_lo)) + rb_ref[...]
    lane = lax.broadcasted_iota(jnp.int32, (tm, LANES), 1)
    logits = jnp.where(lane < N_EXPERTS, logits, NEG_INF)
    m1 = jnp.max(logits, axis=-1, keepdims=True)
    i1 = jnp.min(jnp.where(logits == m1, lane, LANES), axis=-1, keepdims=True)
    rest = jnp.where(lane == i1, NEG_INF, logits)
    m2 = jnp.max(rest, axis=-1, keepdims=True)
    i2 = jnp.min(jnp.where(rest == m2, lane, LANES), axis=-1, keepdims=True)
    e = jnp.exp(m2 - m1)
    w1 = 1.0 / (1.0 + e)
    w2 = e / (1.0 + e)
    oh1, oh2 = lane == i1, lane == i2
    cnt = jnp.where(oh1 | oh2, 1.0, 0.0)
    below = (lax.broadcasted_iota(jnp.int32, (tm, tm), 0) > lax.broadcasted_iota(jnp.int32, (tm, tm), 1))
    prefix = _dot(jnp.where(below, 1.0, 0.0).astype(BF16), cnt.astype(BF16)) + carry[0:1, :]
    r1 = jnp.sum(jnp.where(oh1, prefix, 0.0), axis=-1, keepdims=True)
    r2 = jnp.sum(jnp.where(oh2, prefix, 0.0), axis=-1, keepdims=True)
    meta = jnp.zeros((tm, LANES), F32)
    for k, v in enumerate((i1.astype(F32), i2.astype(F32), w1, w2, r1, r2)):
        meta = jnp.where(lane == k, v, meta)
    meta_ref[...] = meta
    carry[...] += jnp.sum(cnt, axis=0, keepdims=True)
    cnt_ref[...] = carry[...]


def _router(x, ln_g, rw_pad, rb_pad):
    s = x.shape[0]
    tm = ROUTER_TM
    return pl.pallas_call(
        _router_body,
        grid=(s // tm,),
        in_specs=[
            pl.BlockSpec((tm, D_MODEL), lambda i: (i, 0)),
            _const_spec(1, D_MODEL),
            _const_spec(D_MODEL, LANES),
            _const_spec(1, LANES),
        ],
        out_specs=[
            pl.BlockSpec((tm,) + TOKEN_TILE, lambda i: (i, 0, 0)),
            pl.BlockSpec((tm, LANES), lambda i: (i, 0)),
            pl.BlockSpec((SUBLANES, LANES), lambda i: (0, 0)),
        ],
        out_shape=[
            jax.ShapeDtypeStruct((s,) + TOKEN_TILE, F32),
            jax.ShapeDtypeStruct((s, LANES), F32),
            jax.ShapeDtypeStruct((SUBLANES, LANES), F32),
        ],
        scratch_shapes=[pltpu.VMEM((SUBLANES, LANES), F32)],
        compiler_params=pltpu.CompilerParams(
            dimension_semantics=("arbitrary",), vmem_limit_bytes=VMEM_LIMIT_BYTES),
        name="router",
    )(x, ln_g, rw_pad, rb_pad)


def _token_copy(src_ref, src_row, dst_ref, dst_row, sem):
    return pltpu.make_async_copy(src_ref.at[src_row], dst_ref.at[dst_row], sem)


def _gather_body(dest_ref, fill_start_ref, fill_on_ref, h_ref, hs_ref, zero_buf, sem):
    i = pl.program_id(0)

    def fill_copy(k):
        start = pl.multiple_of(fill_start_ref[k], FFN_TM)
        return pltpu.make_async_copy(zero_buf, hs_ref.at[pl.ds(start, FFN_TM)], sem)

    @pl.when(i == 0)
    def _():
        zero_buf[...] = jnp.zeros(zero_buf.shape, F32)
        for k in range(2 * N_EXPERTS):
            @pl.when(fill_on_ref[k] > 0)
            def _():
                fill_copy(k).start()
        for k in range(2 * N_EXPERTS):
            @pl.when(fill_on_ref[k] > 0)
            def _():
                fill_copy(k).wait()

    def start(r, c):
        t = i * GATHER_T + r
        _token_copy(h_ref, r, hs_ref, dest_ref[2 * t], sem).start()
        _token_copy(h_ref, r, hs_ref, dest_ref[2 * t + 1], sem).start()
        return c

    lax.fori_loop(0, GATHER_T, start, 0, unroll=8)

    def wait(r, c):
        _token_copy(h_ref, 0, hs_ref, 0, sem).wait()
        return c

    lax.fori_loop(0, 2 * GATHER_T, wait, 0, unroll=8)


def _gather_rows(dest, fill_start, fill_on, h, rows):
    s = h.shape[0]
    return pl.pallas_call(
        _gather_body,
        grid_spec=pltpu.PrefetchScalarGridSpec(
            num_scalar_prefetch=3,
            grid=(s // GATHER_T,),
            in_specs=[pl.BlockSpec((GATHER_T,) + TOKEN_TILE, lambda i, *_: (i, 0, 0))],
            out_specs=pl.BlockSpec(memory_space=pl.ANY),
            scratch_shapes=[pltpu.VMEM((FFN_TM,) + TOKEN_TILE, F32), pltpu.SemaphoreType.DMA],
        ),
        out_shape=jax.ShapeDtypeStruct((rows,) + TOKEN_TILE, F32),
        compiler_params=pltpu.CompilerParams(dimension_semantics=("arbitrary",)),
        name="gather_rows",
    )(dest, fill_start, fill_on, h)


def _combine_body(dest_ref, x_ref, meta_ref, ys_ref, *rest, final_norm):
    if final_norm:
        g_ref, o_ref, y_buf, sem = rest
    else:
        o_ref, y_buf, sem = rest
    i = pl.program_id(0)

    def start(r, c):
        t = i * GATHER_T + r
        _token_copy(ys_ref, dest_ref[2 * t], y_buf.at[0], r, sem).start()
        _token_copy(ys_ref, dest_ref[2 * t + 1], y_buf.at[1], r, sem).start()
        return c

    lax.fori_loop(0, GATHER_T, start, 0, unroll=8)

    def wait(r, c):
        _token_copy(ys_ref, 0, y_buf.at[0], 0, sem).wait()
        return c

    lax.fori_loop(0, 2 * GATHER_T, wait, 0, unroll=8)

    meta = meta_ref[...]
    lane = lax.broadcasted_iota(jnp.int32, meta.shape, 1)
    w1 = jnp.sum(jnp.where(lane == 2, meta, 0.0), axis=-1, keepdims=True)
    w2 = jnp.sum(jnp.where(lane == 3, meta, 0.0), axis=-1, keepdims=True)
    y1 = y_buf[0].reshape(GATHER_T, D_MODEL)
    y2 = y_buf[1].reshape(GATHER_T, D_MODEL)
    y = x_ref[...] + (w1 * y1 + w2 * y2)
    o_ref[...] = _rmsnorm(y, g_ref[...]) if final_norm else y


def _combine(dest, x, meta, ys, final_g=None):
    s = x.shape[0]
    t = GATHER_T
    in_specs = [
        pl.BlockSpec((t, D_MODEL), lambda i, d: (i, 0)),
        pl.BlockSpec((t, LANES), lambda i, d: (i, 0)),
        pl.BlockSpec(memory_space=pl.ANY),
    ]
    args = [dest, x, meta, ys]
    if final_g is not None:
        in_specs.append(pl.BlockSpec((1, D_MODEL), lambda i, d: (0, 0)))
        args.append(final_g)
    return pl.pallas_call(
        functools.partial(_combine_body, final_norm=final_g is not None),
        grid_spec=pltpu.PrefetchScalarGridSpec(
            num_scalar_prefetch=1,
            grid=(s // t,),
            in_specs=in_specs,
            out_specs=pl.BlockSpec((t, D_MODEL), lambda i, d: (i, 0)),
            scratch_shapes=[pltpu.VMEM((2, t) + TOKEN_TILE, F32), pltpu.SemaphoreType.DMA],
        ),
        out_shape=jax.ShapeDtypeStruct((s, D_MODEL), F32),
        compiler_params=pltpu.CompilerParams(dimension_semantics=("arbitrary",)),
        name="combine",
    )(*args)


def _moe_ffn(x, ln_g, router_w, router_b, wg, wu, wd, final_g):
    s = x.shape[0]
    tm = FFN_TM
    rw_pad = jnp.zeros((D_MODEL, LANES), F32).at[:, :N_EXPERTS].set(router_w)
    rb_pad = jnp.zeros((1, LANES), F32).at[0, :N_EXPERTS].set(router_b)
    h, meta, counts = _router(x, ln_g, rw_pad, rb_pad)
    idx = meta[:, 0:2].astype(jnp.int32)
    rank = meta[:, 4:6].astype(jnp.int32)
    cnt = counts[0, :N_EXPERTS].astype(jnp.int32)
    padded = (cnt + tm - 1) // tm * tm
    ends = jnp.cumsum(padded)
    dest = ((ends - padded)[idx] + rank).reshape(-1)
    n_tiles = 2 * s // tm + N_EXPERTS
    tile_start = jnp.arange(n_tiles, dtype=jnp.int32) * tm
    tile_expert = jnp.minimum(
        jnp.sum(tile_start[:, None] >= ends[None, :], axis=1), N_EXPERTS - 1).astype(jnp.int32)
    n_used = (ends[-1:] // tm).astype(jnp.int32)
    spare = ends[-1] + jnp.arange(N_EXPERTS, dtype=jnp.int32) * tm
    fill_start = jnp.concatenate([ends - tm, spare]).astype(jnp.int32)
    fill_on = jnp.concatenate([padded > 0, spare < n_tiles * tm]).astype(jnp.int32)
    fill_start = jnp.where(fill_on > 0, fill_start, 0)
    hs = _gather_rows(dest, fill_start, fill_on, h, n_tiles * tm)
    ys = _expert_ffn(tile_expert, n_used, hs, wg, wu, wd)
    return _combine(dest, x, meta, ys, final_g)


def kernel(x, ln_mix_g, w_in, attn_sinks, conv_w, pool_w, pool_scale, sgu_w, sgu_b, w_branch, w_out,
           ln_ffn_g, dense_w_gate, dense_w_up, dense_w_down, router_w, router_b,
           moe_w_gate, moe_w_up, moe_w_down, ln_final_g):
    batch, seq, _ = x.shape
    depth = w_in.shape[0]
    assert batch == 1 and seq % MIX_TM == 0 and seq % FFN_TM == 0 and depth % 2 == 0
    assert D_MODEL == SUBLANES * LANES
    xs = x.reshape(seq, D_MODEL)
    for layer in range(depth):
        sgu_b_full = jnp.broadcast_to(sgu_b[layer][:, :, None], (4, GROUP, GROUP))
        xs = _mixer(xs, ln_mix_g[layer][None], w_in[layer].astype(BF16), attn_sinks[layer],
                    conv_w[layer], pool_w[layer].astype(BF16), pool_scale[layer][None],
                    sgu_w[layer].astype(BF16), sgu_b_full, w_branch[layer].astype(BF16),
                    w_out[layer].astype(BF16))
        i = layer // 2
        if layer % 2 == 0:
            xs = _dense_ffn(xs, ln_ffn_g[layer][None], dense_w_gate[i].astype(BF16),
                            dense_w_up[i].astype(BF16), dense_w_down[i].astype(BF16))
        else:
            final_g = ln_final_g[None] if layer == depth - 1 else None
            xs = _moe_ffn(xs, ln_ffn_g[layer][None], router_w[i], router_b[i],
                          moe_w_gate[i].astype(BF16), moe_w_up[i].astype(BF16),
                          moe_w_down[i].astype(BF16), final_g)
    return xs.reshape(batch, seq, D_MODEL)
```

```python
import functools

import jax
import jax.numpy as jnp
from jax import lax
from jax.experimental import pallas as pl
from jax.experimental.pallas import tpu as pltpu

F32 = jnp.float32
BF16 = jnp.bfloat16

D_MODEL = 1024
ATT_HEADS = 8
ATT_KV_HEADS = 2
HEAD_DIM = 64
ATT_BLOCK = 128
POOL_SIZES = (2, 4, 8, 16)
GROUP = 128
N_BRANCH = 4
BRANCH_WIDTH = 512
FFN_HIDDEN = 2816
N_EXPERTS = 8
RMS_EPS = 1e-6
NEG_INF = -1e30

_C_Q, _C_KV, _C_CONV, _C_POOL, _C_SGU, _C_GATE = 0, 512, 768, 2304, 2816, 3840
IN_COLS = _C_GATE + N_BRANCH * D_MODEL

LANES = 128
SUBLANES = 8
TOKEN_TILE = (SUBLANES, LANES)
MXU_COLS = 256
VMEM_LIMIT_BYTES = 60 * 1024 * 1024

MIX_TM = 512
FFN_TM = 512
FFN_FC = MXU_COLS
ROUTER_TM = 512
GATHER_T = 512
COMBINE_T = 256
CONV_HALO = SUBLANES
POOL_HALO = 2 * SUBLANES


def _rmsnorm(x, g):
    return x * lax.rsqrt(jnp.mean(x * x, axis=-1, keepdims=True) + RMS_EPS) * g


def _sigmoid(x):
    return 1.0 / (1.0 + jnp.exp(-x))


def _dot(a, b):
    return jnp.dot(a, b, preferred_element_type=F32)


def _const_spec(*shape):
    return pl.BlockSpec(shape, lambda *_: (0,) * len(shape), pipeline_mode=pl.Buffered(1))


def _layer_spec(layer, *shape):
    return pl.BlockSpec((None,) + shape, lambda *_: (layer,) + (0,) * len(shape),
                        pipeline_mode=pl.Buffered(1))


def _mixer_body(sinks_ref, x_ref, g_ref, w_in_ref, conv_w_ref, pool_w_ref, pool_scale_ref,
                sgu_w_ref, sgu_b_ref, w_br_ref, w_out_ref, o_ref,
                kv_buf, z_buf, p_buf, y_buf, gate_buf):
    tm = x_ref.shape[0]
    nblk = tm // ATT_BLOCK
    i = pl.program_id(0)

    @pl.when(i == 0)
    def _():
        kv_buf[pl.ds(0, ATT_BLOCK), :] = jnp.zeros((ATT_BLOCK, kv_buf.shape[1]), F32)
        z_buf[pl.ds(0, CONV_HALO), :] = jnp.zeros((CONV_HALO, z_buf.shape[1]), F32)
        p_buf[pl.ds(0, POOL_HALO), :] = jnp.zeros((POOL_HALO, p_buf.shape[1]), F32)

    x = x_ref[...]
    hb = _rmsnorm(x, g_ref[...]).astype(BF16)

    def proj(c0, n):
        return _dot(hb, w_in_ref[:, c0:c0 + n])

    gate_chunks = [(n, c) for n in range(N_BRANCH) for c in range(D_MODEL // MXU_COLS)]

    def gate_chunk():
        n, c = gate_chunks.pop(0)
        cols = slice(c * MXU_COLS, (c + 1) * MXU_COLS)
        gate_buf[n, :, cols] = _sigmoid(proj(_C_GATE + n * D_MODEL + c * MXU_COLS, MXU_COLS))

    qkv = proj(_C_Q, 768)
    kv_buf[pl.ds(ATT_BLOCK, tm), :] = qkv[:, 512:768]
    q = qkv[:, 0:512] * (HEAD_DIM ** -0.5)
    c3 = proj(_C_CONV, 3 * BRANCH_WIDTH)
    pp = proj(_C_POOL, BRANCH_WIDTH)
    usv = proj(_C_SGU, 2 * BRANCH_WIDTH)

    z = c3[:, 512:1024] * c3[:, 1024:1536]
    z_buf[pl.ds(CONV_HALO, tm), :] = z
    cw = conv_w_ref[...]
    conv = (cw[0:1] * z_buf[pl.ds(CONV_HALO - 2, tm), :] + cw[1:2] * z_buf[pl.ds(CONV_HALO - 1, tm), :]
            + cw[2:3] * z)
    z_buf[pl.ds(0, CONV_HALO), :] = z_buf[pl.ds(tm, CONV_HALO), :]
    y_buf[1] = (c3[:, 0:512] * conv).astype(BF16)

    p_buf[pl.ds(POOL_HALO, tm), :] = pp
    tpos = i * tm + lax.broadcasted_iota(jnp.int32, (tm, GROUP), 0)
    pooled = []
    for gi, w in enumerate(POOL_SIZES):
        cs = slice(gi * GROUP, (gi + 1) * GROUP)
        win = pp[:, cs]
        for k in range(1, w):
            win = win + p_buf[pl.ds(POOL_HALO - k, tm), cs]
        cnt = jnp.minimum(tpos + 1, w).astype(F32)
        pooled.append((win / cnt - pp[:, cs]).astype(BF16))
    p_buf[pl.ds(0, POOL_HALO), :] = p_buf[pl.ds(tm, POOL_HALO), :]

    def pool_linear():
        for gi in range(len(POOL_SIZES)):
            cs = slice(gi * GROUP, (gi + 1) * GROUP)
            y_buf[2, :, cs] = (_dot(pooled[gi], pool_w_ref[gi]) * pool_scale_ref[:, cs]).astype(BF16)

    def sgu_mix():
        tril = (lax.broadcasted_iota(jnp.int32, (GROUP, GROUP), 0)
                >= lax.broadcasted_iota(jnp.int32, (GROUP, GROUP), 1))
        for g in range(BRANCH_WIDTH // GROUP):
            wg = jnp.where(tril, sgu_w_ref[g], jnp.zeros((GROUP, GROUP), BF16))
            for c in range(nblk):
                rows = slice(c * GROUP, (c + 1) * GROUP)
                sv = usv[rows, BRANCH_WIDTH + g * GROUP:BRANCH_WIDTH + (g + 1) * GROUP].astype(BF16)
                mixed = _dot(wg, sv) + sgu_b_ref[g]
                y_buf[3, pl.ds(c * GROUP, GROUP), g * GROUP:(g + 1) * GROUP] = (
                    usv[rows, g * GROUP:(g + 1) * GROUP] * mixed).astype(BF16)

    lo128 = lax.broadcasted_iota(jnp.int32, (ATT_BLOCK, LANES), 1) < HEAD_DIM
    lo256 = lax.broadcasted_iota(jnp.int32, (2 * ATT_BLOCK, LANES), 1) < HEAD_DIM
    qi = lax.broadcasted_iota(jnp.int32, (ATT_BLOCK, 2 * ATT_BLOCK), 0)
    kj = lax.broadcasted_iota(jnp.int32, (ATT_BLOCK, 2 * ATT_BLOCK), 1)
    rel = qi + ATT_BLOCK - kj
    local = (rel >= 0) & (rel < ATT_BLOCK)

    def attention_operands(b):
        kvb = kv_buf[pl.ds(b * ATT_BLOCK, 2 * ATT_BLOCK), :]
        kb, vb = kvb[:, 0:LANES], kvb[:, LANES:2 * LANES]
        kr, vr = pltpu.roll(kb, HEAD_DIM, 1), pltpu.roll(vb, HEAD_DIM, 1)
        kkT = [jnp.where(lo256, kb, kr).T.astype(BF16), jnp.where(lo256, kr, kb).T.astype(BF16)]
        vv = [jnp.where(lo256, vb, vr).astype(BF16), jnp.where(lo256, vr, vb).astype(BF16)]
        valid = local & (kj + (i * tm + (b - 1) * ATT_BLOCK) >= 0)
        return kkT, vv, valid

    operands = attention_operands(0)
    for b in range(nblk):
        kkT, vv, valid = operands
        for j in range(ATT_HEADS // 2):
            qp = q[b * ATT_BLOCK:(b + 1) * ATT_BLOCK, j * LANES:(j + 1) * LANES]
            kvh = (2 * j) // (ATT_HEADS // ATT_KV_HEADS)
            scores = []
            for half in range(2):
                keep = lo128 if half == 0 else jnp.logical_not(lo128)
                qm = jnp.where(keep, qp, 0.0).astype(BF16)
                scores.append(jnp.where(valid, _dot(qm, kkT[kvh]), NEG_INF))
            if j == 1 and b + 1 < nblk:
                operands = attention_operands(b + 1)
            gate_chunk()
            outs = []
            for half in range(2):
                s = scores[half]
                sink = sinks_ref[2 * j + half]
                m = jnp.maximum(jnp.max(s, axis=-1, keepdims=True), sink)
                p = jnp.exp(s - m)
                den = jnp.sum(p, axis=-1, keepdims=True) + jnp.exp(sink - m)
                outs.append(_dot(p.astype(BF16), vv[kvh]) / den)
            y_buf[0, pl.ds(b * ATT_BLOCK, ATT_BLOCK), j * LANES:(j + 1) * LANES] = (
                jnp.where(lo128, outs[0], outs[1]).astype(BF16))
        if b == 0:
            pool_linear()
        if b == 1:
            sgu_mix()
    kv_buf[pl.ds(0, ATT_BLOCK), :] = kv_buf[pl.ds(tm, ATT_BLOCK), :]
    while gate_chunks:
        gate_chunk()

    merged = gate_buf[0] * _dot(y_buf[0], w_br_ref[0])
    for n in range(1, N_BRANCH):
        merged = merged + gate_buf[n] * _dot(y_buf[n], w_br_ref[n])
    o_ref[...] = x + _dot(merged.astype(BF16), w_out_ref[...])


def _mixer(layer, x, ln_g, w_in, sinks, conv_w, pool_w, pool_scale, sgu_w, sgu_b_full, w_br, w_out):
    s = x.shape[0]
    tm = MIX_TM
    return pl.pallas_call(
        _mixer_body,
        grid=(s // tm,),
        in_specs=[
            pl.BlockSpec(memory_space=pltpu.SMEM),
            pl.BlockSpec((tm, D_MODEL), lambda i: (i, 0)),
            _const_spec(1, D_MODEL),
            _layer_spec(layer, D_MODEL, IN_COLS),
            _const_spec(3, BRANCH_WIDTH),
            _layer_spec(layer, 4, GROUP, GROUP),
            _const_spec(1, BRANCH_WIDTH),
            _layer_spec(layer, 4, GROUP, GROUP),
            _const_spec(4, GROUP, GROUP),
            _layer_spec(layer, N_BRANCH, BRANCH_WIDTH, D_MODEL),
            _layer_spec(layer, D_MODEL, D_MODEL),
        ],
        out_specs=pl.BlockSpec((tm, D_MODEL), lambda i: (i, 0)),
        out_shape=jax.ShapeDtypeStruct((s, D_MODEL), F32),
        scratch_shapes=[
            pltpu.VMEM((tm + ATT_BLOCK, 2 * LANES), F32),
            pltpu.VMEM((tm + CONV_HALO, BRANCH_WIDTH), F32),
            pltpu.VMEM((tm + POOL_HALO, BRANCH_WIDTH), F32),
            pltpu.VMEM((N_BRANCH, tm, BRANCH_WIDTH), BF16),
            pltpu.VMEM((N_BRANCH, tm, D_MODEL), F32),
        ],
        compiler_params=pltpu.CompilerParams(
            dimension_semantics=("arbitrary",), vmem_limit_bytes=VMEM_LIMIT_BYTES),
        name="mixer",
    )(sinks, x, ln_g, w_in, conv_w, pool_w, pool_scale, sgu_w, sgu_b_full, w_br, w_out)


def _swiglu_into(hb, wg_ref, wu_ref, wd_ref, acc_ref, base):
    for c in range(FFN_HIDDEN // FFN_FC):
        cs = slice(c * FFN_FC, (c + 1) * FFN_FC)
        g = _dot(hb, wg_ref[:, cs])
        u = _dot(hb, wu_ref[:, cs])
        a = (g * _sigmoid(g) * u).astype(BF16)
        d = _dot(a, wd_ref[cs, :])
        if c == 0:
            acc_ref[...] = d if base is None else base + d
        else:
            acc_ref[...] += d


def _dense_ffn_body(x_ref, g_ref, wg_ref, wu_ref, wd_ref, o_ref):
    x = x_ref[...]
    hb = _rmsnorm(x, g_ref[...]).astype(BF16)
    _swiglu_into(hb, wg_ref, wu_ref, wd_ref, o_ref, x)


def _dense_ffn(layer, x, ln_g, wg, wu, wd):
    s = x.shape[0]
    tm = FFN_TM
    return pl.pallas_call(
        _dense_ffn_body,
        grid=(s // tm,),
        in_specs=[
            pl.BlockSpec((tm, D_MODEL), lambda i: (i, 0)),
            _const_spec(1, D_MODEL),
            _layer_spec(layer, D_MODEL, FFN_HIDDEN),
            _layer_spec(layer, D_MODEL, FFN_HIDDEN),
            _layer_spec(layer, FFN_HIDDEN, D_MODEL),
        ],
        out_specs=pl.BlockSpec((tm, D_MODEL), lambda i: (i, 0)),
        out_shape=jax.ShapeDtypeStruct((s, D_MODEL), F32),
        compiler_params=pltpu.CompilerParams(
            dimension_semantics=("arbitrary",), vmem_limit_bytes=VMEM_LIMIT_BYTES),
        name="dense_ffn",
    )(x, ln_g, wg, wu, wd)


def _expert_ffn_body(tile_expert_ref, n_used_ref, hs_ref, wg_ref, wu_ref, wd_ref, o_ref, acc_ref):
    del tile_expert_ref
    i = pl.program_id(0)
    tm = hs_ref.shape[0]

    @pl.when(i < n_used_ref[0])
    def _():
        hb = hs_ref[...].reshape(tm, D_MODEL).astype(BF16)
        _swiglu_into(hb, wg_ref, wu_ref, wd_ref, acc_ref, None)
        o_ref[...] = acc_ref[...].reshape(o_ref.shape)

    @pl.when(i >= n_used_ref[0])
    def _():
        o_ref[...] = jnp.zeros(o_ref.shape, F32)


def _expert_ffn(layer, tile_expert, n_used, hs, wg, wu, wd):
    rows = hs.shape[0]
    tm = FFN_TM
    w_in_spec = pl.BlockSpec((None, None, D_MODEL, FFN_HIDDEN), lambda i, te, nu: (layer, te[i], 0, 0))
    w_out_spec = pl.BlockSpec((None, None, FFN_HIDDEN, D_MODEL), lambda i, te, nu: (layer, te[i], 0, 0))
    return pl.pallas_call(
        _expert_ffn_body,
        grid_spec=pltpu.PrefetchScalarGridSpec(
            num_scalar_prefetch=2,
            grid=(rows // tm,),
            in_specs=[
                pl.BlockSpec((tm,) + TOKEN_TILE, lambda i, te, nu: (jnp.minimum(i, nu[0] - 1), 0, 0)),
                w_in_spec, w_in_spec, w_out_spec,
            ],
            out_specs=pl.BlockSpec((tm,) + TOKEN_TILE, lambda i, te, nu: (i, 0, 0)),
            scratch_shapes=[pltpu.VMEM((tm, D_MODEL), F32)],
        ),
        out_shape=jax.ShapeDtypeStruct((rows,) + TOKEN_TILE, F32),
        compiler_params=pltpu.CompilerParams(
            dimension_semantics=("arbitrary",), vmem_limit_bytes=VMEM_LIMIT_BYTES),
        name="expert_ffn",
    )(tile_expert, n_used, hs, wg, wu, wd)


def _router_body(x_ref, g_ref, rw_ref, rb_ref, h_ref, meta_ref, cnt_ref, carry):
    tm = x_ref.shape[0]
    i = pl.program_id(0)

    @pl.when(i == 0)
    def _():
        carry[...] = jnp.zeros(carry.shape, F32)

    h = _rmsnorm(x_ref[...], g_ref[...])
    h_ref[...] = h.reshape(h_ref.shape)
    h_hi = h.astype(BF16)
    h_lo = (h - h_hi.astype(F32)).astype(BF16)
    w = rw_ref[...]
    w_hi = w.astype(BF16)
    w_lo = (w - w_hi.astype(F32)).astype(BF16)
    logits = _dot(h_hi, w_hi) + (_dot(h_lo, w_hi) + _dot(h_hi, w_lo)) + rb_ref[...]
    lane = lax.broadcasted_iota(jnp.int32, (tm, LANES), 1)
    logits = jnp.where(lane < N_EXPERTS, logits, NEG_INF)
    m1 = jnp.max(logits, axis=-1, keepdims=True)
    i1 = jnp.min(jnp.where(logits == m1, lane, LANES), axis=-1, keepdims=True)
    rest = jnp.where(lane == i1, NEG_INF, logits)
    m2 = jnp.max(rest, axis=-1, keepdims=True)
    i2 = jnp.min(jnp.where(rest == m2, lane, LANES), axis=-1, keepdims=True)
    e = jnp.exp(m2 - m1)
    w1 = 1.0 / (1.0 + e)
    w2 = e / (1.0 + e)
    oh1, oh2 = lane == i1, lane == i2
    cnt = jnp.where(oh1 | oh2, 1.0, 0.0)
    below = (lax.broadcasted_iota(jnp.int32, (tm, tm), 0) > lax.broadcasted_iota(jnp.int32, (tm, tm), 1))
    prefix = _dot(jnp.where(below, 1.0, 0.0).astype(BF16), cnt.astype(BF16)) + carry[0:1, :]
    r1 = jnp.sum(jnp.where(oh1, prefix, 0.0), axis=-1, keepdims=True)
    r2 = jnp.sum(jnp.where(oh2, prefix, 0.0), axis=-1, keepdims=True)
    meta = jnp.zeros((tm, LANES), F32)
    for k, v in enumerate((i1.astype(F32), i2.astype(F32), w1, w2, r1, r2)):
        meta = jnp.where(lane == k, v, meta)
    meta_ref[...] = meta
    carry[...] += jnp.sum(cnt, axis=0, keepdims=True)
    cnt_ref[...] = carry[...]


def _router(x, ln_g, rw_pad, rb_pad):
    s = x.shape[0]
    tm = ROUTER_TM
    return pl.pallas_call(
        _router_body,
        grid=(s // tm,),
        in_specs=[
            pl.BlockSpec((tm, D_MODEL), lambda i: (i, 0)),
            _const_spec(1, D_MODEL),
            _const_spec(D_MODEL, LANES),
            _const_spec(1, LANES),
        ],
        out_specs=[
            pl.BlockSpec((tm,) + TOKEN_TILE, lambda i: (i, 0, 0)),
            pl.BlockSpec((tm, LANES), lambda i: (i, 0)),
            pl.BlockSpec((SUBLANES, LANES), lambda i: (0, 0)),
        ],
        out_shape=[
            jax.ShapeDtypeStruct((s,) + TOKEN_TILE, F32),
            jax.ShapeDtypeStruct((s, LANES), F32),
            jax.ShapeDtypeStruct((SUBLANES, LANES), F32),
        ],
        scratch_shapes=[pltpu.VMEM((SUBLANES, LANES), F32)],
        compiler_params=pltpu.CompilerParams(
            dimension_semantics=("arbitrary",), vmem_limit_bytes=VMEM_LIMIT_BYTES),
        name="router",
    )(x, ln_g, rw_pad, rb_pad)


def _token_copy(src_ref, src_row, dst_ref, dst_row, sem):
    return pltpu.make_async_copy(src_ref.at[src_row], dst_ref.at[dst_row], sem)


def _gather_body(dest_ref, fill_start_ref, fill_on_ref, h_ref, hs_ref, zero_buf, sem):
    i = pl.program_id(0)

    def fill_copy(k):
        start = pl.multiple_of(fill_start_ref[k], FFN_TM)
        return pltpu.make_async_copy(zero_buf, hs_ref.at[pl.ds(start, FFN_TM)], sem)

    @pl.when(i == 0)
    def _():
        zero_buf[...] = jnp.zeros(zero_buf.shape, F32)
        for k in range(2 * N_EXPERTS):
            @pl.when(fill_on_ref[k] > 0)
            def _():
                fill_copy(k).start()
        for k in range(2 * N_EXPERTS):
            @pl.when(fill_on_ref[k] > 0)
            def _():
                fill_copy(k).wait()

    def start(r, c):
        t = i * GATHER_T + r
        _token_copy(h_ref, r, hs_ref, dest_ref[2 * t], sem).start(priority=0)
        _token_copy(h_ref, r, hs_ref, dest_ref[2 * t + 1], sem).start(priority=1)
        return c

    lax.fori_loop(0, GATHER_T, start, 0, unroll=8)

    def wait(r, c):
        _token_copy(h_ref, 0, hs_ref, 0, sem).wait()
        return c

    lax.fori_loop(0, 2 * GATHER_T, wait, 0, unroll=8)


def _gather_rows(dest, fill_start, fill_on, h, rows):
    s = h.shape[0]
    return pl.pallas_call(
        _gather_body,
        grid_spec=pltpu.PrefetchScalarGridSpec(
            num_scalar_prefetch=3,
            grid=(s // GATHER_T,),
            in_specs=[pl.BlockSpec((GATHER_T,) + TOKEN_TILE, lambda i, *_: (i, 0, 0))],
            out_specs=pl.BlockSpec(memory_space=pl.ANY),
            scratch_shapes=[pltpu.VMEM((FFN_TM,) + TOKEN_TILE, F32), pltpu.SemaphoreType.DMA],
        ),
        out_shape=jax.ShapeDtypeStruct((rows,) + TOKEN_TILE, F32),
        compiler_params=pltpu.CompilerParams(dimension_semantics=("arbitrary",)),
        name="gather_rows",
    )(dest, fill_start, fill_on, h)


def _combine_body(dest_ref, x_ref, meta_ref, ys_ref, *rest, final_norm):
    if final_norm:
        g_ref, o_ref, y_buf, sems = rest
    else:
        o_ref, y_buf, sems = rest
    i = pl.program_id(0)
    slot = i % 2

    def issue(step, slot_):
        def start(r, c):
            t = step * COMBINE_T + r
            _token_copy(ys_ref, dest_ref[2 * t], y_buf.at[slot_, 0], r, sems.at[slot_]).start(priority=0)
            _token_copy(ys_ref, dest_ref[2 * t + 1], y_buf.at[slot_, 1], r, sems.at[slot_]).start(priority=1)
            return c

        lax.fori_loop(0, COMBINE_T, start, 0, unroll=8)

    @pl.when(i == 0)
    def _():
        issue(0, 0)

    @pl.when(i + 1 < pl.num_programs(0))
    def _():
        issue(i + 1, 1 - slot)

    def wait(r, c):
        _token_copy(ys_ref, 0, y_buf.at[slot, 0], 0, sems.at[slot]).wait()
        return c

    lax.fori_loop(0, 2 * COMBINE_T, wait, 0, unroll=8)

    meta = meta_ref[...]
    lane = lax.broadcasted_iota(jnp.int32, meta.shape, 1)
    w1 = jnp.sum(jnp.where(lane == 2, meta, 0.0), axis=-1, keepdims=True)
    w2 = jnp.sum(jnp.where(lane == 3, meta, 0.0), axis=-1, keepdims=True)
    y1 = y_buf[slot, 0].reshape(COMBINE_T, D_MODEL)
    y2 = y_buf[slot, 1].reshape(COMBINE_T, D_MODEL)
    y = x_ref[...] + (w1 * y1 + w2 * y2)
    o_ref[...] = _rmsnorm(y, g_ref[...]) if final_norm else y


def _combine(dest, x, meta, ys, final_g=None):
    s = x.shape[0]
    t = COMBINE_T
    in_specs = [
        pl.BlockSpec((t, D_MODEL), lambda i, d: (i, 0)),
        pl.BlockSpec((t, LANES), lambda i, d: (i, 0)),
        pl.BlockSpec(memory_space=pl.ANY),
    ]
    args = [dest, x, meta, ys]
    if final_g is not None:
        in_specs.append(pl.BlockSpec((1, D_MODEL), lambda i, d: (0, 0)))
        args.append(final_g)
    return pl.pallas_call(
        functools.partial(_combine_body, final_norm=final_g is not None),
        grid_spec=pltpu.PrefetchScalarGridSpec(
            num_scalar_prefetch=1,
            grid=(s // t,),
            in_specs=in_specs,
            out_specs=pl.BlockSpec((t, D_MODEL), lambda i, d: (i, 0)),
            scratch_shapes=[pltpu.VMEM((2, 2, t) + TOKEN_TILE, F32), pltpu.SemaphoreType.DMA((2,))],
        ),
        out_shape=jax.ShapeDtypeStruct((s, D_MODEL), F32),
        compiler_params=pltpu.CompilerParams(dimension_semantics=("arbitrary",)),
        name="combine",
    )(*args)


def _moe_ffn(layer, x, ln_g, router_w, router_b, wg, wu, wd, final_g):
    s = x.shape[0]
    tm = FFN_TM
    rw_pad = jnp.zeros((D_MODEL, LANES), F32).at[:, :N_EXPERTS].set(router_w)
    rb_pad = jnp.zeros((1, LANES), F32).at[0, :N_EXPERTS].set(router_b)
    h, meta, counts = _router(x, ln_g, rw_pad, rb_pad)
    idx = meta[:, 0:2].astype(jnp.int32)
    rank = meta[:, 4:6].astype(jnp.int32)
    cnt = counts[0, :N_EXPERTS].astype(jnp.int32)
    padded = (cnt + tm - 1) // tm * tm
    ends = jnp.cumsum(padded)
    dest = ((ends - padded)[idx] + rank).reshape(-1)
    n_tiles = 2 * s // tm + N_EXPERTS
    tile_start = jnp.arange(n_tiles, dtype=jnp.int32) * tm
    tile_expert = jnp.minimum(
        jnp.sum(tile_start[:, None] >= ends[None, :], axis=1), N_EXPERTS - 1).astype(jnp.int32)
    n_used = (ends[-1:] // tm).astype(jnp.int32)
    spare = ends[-1] + jnp.arange(N_EXPERTS, dtype=jnp.int32) * tm
    fill_start = jnp.concatenate([ends - tm, spare]).astype(jnp.int32)
    fill_on = jnp.concatenate([padded > 0, spare < n_tiles * tm]).astype(jnp.int32)
    fill_start = jnp.where(fill_on > 0, fill_start, 0)
    hs = _gather_rows(dest, fill_start, fill_on, h, n_tiles * tm)
    ys = _expert_ffn(layer, tile_expert, n_used, hs, wg, wu, wd)
    return _combine(dest, x, meta, ys, final_g)


def kernel(x, ln_mix_g, w_in, attn_sinks, conv_w, pool_w, pool_scale, sgu_w, sgu_b, w_branch, w_out,
           ln_ffn_g, dense_w_gate, dense_w_up, dense_w_down, router_w, router_b,
           moe_w_gate, moe_w_up, moe_w_down, ln_final_g):
    batch, seq, _ = x.shape
    depth = w_in.shape[0]
    assert batch == 1 and seq % MIX_TM == 0 and seq % FFN_TM == 0 and depth % 2 == 0
    assert D_MODEL == SUBLANES * LANES
    xs = x.reshape(seq, D_MODEL)
    w_in, pool_w, sgu_w, w_branch, w_out = (w.astype(BF16) for w in (w_in, pool_w, sgu_w, w_branch, w_out))
    dense_w = [w.astype(BF16) for w in (dense_w_gate, dense_w_up, dense_w_down)]
    moe_w = [w.astype(BF16) for w in (moe_w_gate, moe_w_up, moe_w_down)]
    for layer in range(depth):
        sgu_b_full = jnp.broadcast_to(sgu_b[layer][:, :, None], (4, GROUP, GROUP))
        xs = _mixer(layer, xs, ln_mix_g[layer][None], w_in, attn_sinks[layer], conv_w[layer], pool_w,
                    pool_scale[layer][None], sgu_w, sgu_b_full, w_branch, w_out)
        i = layer // 2
        if layer % 2 == 0:
            xs = _dense_ffn(i, xs, ln_ffn_g[layer][None], *dense_w)
        else:
            final_g = ln_final_g[None] if layer == depth - 1 else None
            xs = _moe_ffn(i, xs, ln_ffn_g[layer][None], router_w[i], router_b[i], *moe_w, final_g)
    return xs.reshape(batch, seq, D_MODEL)
```

```python
import functools

import jax
import jax.numpy as jnp
from jax import lax
from jax.experimental import pallas as pl
from jax.experimental.pallas import tpu as pltpu

F32 = jnp.float32
BF16 = jnp.bfloat16

D_MODEL = 1024
ATT_HEADS = 8
ATT_KV_HEADS = 2
HEAD_DIM = 64
ATT_BLOCK = 128
POOL_SIZES = (2, 4, 8, 16)
GROUP = 128
N_BRANCH = 4
BRANCH_WIDTH = 512
FFN_HIDDEN = 2816
N_EXPERTS = 8
RMS_EPS = 1e-6
NEG_INF = -1e30

_C_Q, _C_KV, _C_CONV, _C_POOL, _C_SGU, _C_GATE = 0, 512, 768, 2304, 2816, 3840
IN_COLS = _C_GATE + N_BRANCH * D_MODEL

LANES = 128
SUBLANES = 8
TOKEN_TILE = (SUBLANES, LANES)
MXU_COLS = 256
VMEM_LIMIT_BYTES = 60 * 1024 * 1024

MIX_TM = 512
FFN_TM = 512
FFN_FC = MXU_COLS
ROUTER_TM = 512
GATHER_T = 512
COMBINE_T = 256
CONV_HALO = SUBLANES
POOL_HALO = 2 * SUBLANES


def _rmsnorm(x, g):
    return x * lax.rsqrt(jnp.mean(x * x, axis=-1, keepdims=True) + RMS_EPS) * g


def _sigmoid(x):
    return 1.0 / (1.0 + jnp.exp(-x))


def _dot(a, b):
    return jnp.dot(a, b, preferred_element_type=F32)


def _const_spec(*shape):
    return pl.BlockSpec(shape, lambda *_: (0,) * len(shape), pipeline_mode=pl.Buffered(1))


def _layer_spec(layer, *shape):
    return pl.BlockSpec((None,) + shape, lambda *_: (layer,) + (0,) * len(shape),
                        pipeline_mode=pl.Buffered(1))


def _mixer_body(sinks_ref, x_ref, g_ref, w_in_ref, conv_w_ref, pool_w_ref, pool_scale_ref,
                sgu_w_ref, sgu_b_ref, w_br_ref, w_out_ref, o_ref,
                kv_buf, z_buf, p_buf, y_buf, gate_buf):
    tm = x_ref.shape[0]
    nblk = tm // ATT_BLOCK
    i = pl.program_id(0)

    @pl.when(i == 0)
    def _():
        kv_buf[pl.ds(0, ATT_BLOCK), :] = jnp.zeros((ATT_BLOCK, kv_buf.shape[1]), F32)
        z_buf[pl.ds(0, CONV_HALO), :] = jnp.zeros((CONV_HALO, z_buf.shape[1]), F32)
        p_buf[pl.ds(0, POOL_HALO), :] = jnp.zeros((POOL_HALO, p_buf.shape[1]), F32)

    x = x_ref[...]
    hb = _rmsnorm(x, g_ref[...]).astype(BF16)

    def proj(c0, n):
        return _dot(hb, w_in_ref[:, c0:c0 + n])

    gate_chunks = [(n, c) for n in range(N_BRANCH) for c in range(D_MODEL // MXU_COLS)]

    def gate_chunk():
        n, c = gate_chunks.pop(0)
        cols = slice(c * MXU_COLS, (c + 1) * MXU_COLS)
        gate_buf[n, :, cols] = _sigmoid(proj(_C_GATE + n * D_MODEL + c * MXU_COLS, MXU_COLS))

    qkv = proj(_C_Q, 768)
    kv_buf[pl.ds(ATT_BLOCK, tm), :] = qkv[:, 512:768]
    q = qkv[:, 0:512] * (HEAD_DIM ** -0.5)
    c3 = proj(_C_CONV, 3 * BRANCH_WIDTH)
    pp = proj(_C_POOL, BRANCH_WIDTH)
    usv = proj(_C_SGU, 2 * BRANCH_WIDTH)

    z = c3[:, 512:1024] * c3[:, 1024:1536]
    z_buf[pl.ds(CONV_HALO, tm), :] = z
    cw = conv_w_ref[...]
    conv = (cw[0:1] * z_buf[pl.ds(CONV_HALO - 2, tm), :] + cw[1:2] * z_buf[pl.ds(CONV_HALO - 1, tm), :]
            + cw[2:3] * z)
    z_buf[pl.ds(0, CONV_HALO), :] = z_buf[pl.ds(tm, CONV_HALO), :]
    y_buf[1] = (c3[:, 0:512] * conv).astype(BF16)

    p_buf[pl.ds(POOL_HALO, tm), :] = pp
    tpos = i * tm + lax.broadcasted_iota(jnp.int32, (tm, GROUP), 0)
    pooled = []
    for gi, w in enumerate(POOL_SIZES):
        cs = slice(gi * GROUP, (gi + 1) * GROUP)
        win = pp[:, cs]
        for k in range(1, w):
            win = win + p_buf[pl.ds(POOL_HALO - k, tm), cs]
        cnt = jnp.minimum(tpos + 1, w).astype(F32)
        pooled.append((win / cnt - pp[:, cs]).astype(BF16))
    p_buf[pl.ds(0, POOL_HALO), :] = p_buf[pl.ds(tm, POOL_HALO), :]

    def pool_linear():
        for gi in range(len(POOL_SIZES)):
            cs = slice(gi * GROUP, (gi + 1) * GROUP)
            y_buf[2, :, cs] = (_dot(pooled[gi], pool_w_ref[gi]) * pool_scale_ref[:, cs]).astype(BF16)

    def sgu_mix():
        tril = (lax.broadcasted_iota(jnp.int32, (GROUP, GROUP), 0)
                >= lax.broadcasted_iota(jnp.int32, (GROUP, GROUP), 1))
        for g in range(BRANCH_WIDTH // GROUP):
            wg = jnp.where(tril, sgu_w_ref[g], jnp.zeros((GROUP, GROUP), BF16))
            for c in range(nblk):
                rows = slice(c * GROUP, (c + 1) * GROUP)
                sv = usv[rows, BRANCH_WIDTH + g * GROUP:BRANCH_WIDTH + (g + 1) * GROUP].astype(BF16)
                mixed = _dot(wg, sv) + sgu_b_ref[g]
                y_buf[3, pl.ds(c * GROUP, GROUP), g * GROUP:(g + 1) * GROUP] = (
                    usv[rows, g * GROUP:(g + 1) * GROUP] * mixed).astype(BF16)

    lo128 = lax.broadcasted_iota(jnp.int32, (ATT_BLOCK, LANES), 1) < HEAD_DIM
    lo256 = lax.broadcasted_iota(jnp.int32, (2 * ATT_BLOCK, LANES), 1) < HEAD_DIM
    qi = lax.broadcasted_iota(jnp.int32, (ATT_BLOCK, 2 * ATT_BLOCK), 0)
    kj = lax.broadcasted_iota(jnp.int32, (ATT_BLOCK, 2 * ATT_BLOCK), 1)
    rel = qi + ATT_BLOCK - kj
    local = (rel >= 0) & (rel < ATT_BLOCK)

    def attention_operands(b):
        kvb = kv_buf[pl.ds(b * ATT_BLOCK, 2 * ATT_BLOCK), :]
        kb, vb = kvb[:, 0:LANES], kvb[:, LANES:2 * LANES]
        kr, vr = pltpu.roll(kb, HEAD_DIM, 1), pltpu.roll(vb, HEAD_DIM, 1)
        kkT = [jnp.where(lo256, kb, kr).T.astype(BF16), jnp.where(lo256, kr, kb).T.astype(BF16)]
        vv = [jnp.where(lo256, vb, vr).astype(BF16), jnp.where(lo256, vr, vb).astype(BF16)]
        valid = local & (kj + (i * tm + (b - 1) * ATT_BLOCK) >= 0)
        return kkT, vv, valid

    operands = attention_operands(0)
    for b in range(nblk):
        kkT, vv, valid = operands
        for j in range(ATT_HEADS // 2):
            qp = q[b * ATT_BLOCK:(b + 1) * ATT_BLOCK, j * LANES:(j + 1) * LANES]
            kvh = (2 * j) // (ATT_HEADS // ATT_KV_HEADS)
            scores = []
            for half in range(2):
                keep = lo128 if half == 0 else jnp.logical_not(lo128)
                qm = jnp.where(keep, qp, 0.0).astype(BF16)
                scores.append(jnp.where(valid, _dot(qm, kkT[kvh]), NEG_INF))
            if j == 1 and b + 1 < nblk:
                operands = attention_operands(b + 1)
            gate_chunk()
            outs = []
            for half in range(2):
                s = scores[half]
                sink = sinks_ref[2 * j + half]
                m = jnp.maximum(jnp.max(s, axis=-1, keepdims=True), sink)
                p = jnp.exp(s - m)
                den = jnp.sum(p, axis=-1, keepdims=True) + jnp.exp(sink - m)
                outs.append(_dot(p.astype(BF16), vv[kvh]) / den)
            y_buf[0, pl.ds(b * ATT_BLOCK, ATT_BLOCK), j * LANES:(j + 1) * LANES] = (
                jnp.where(lo128, outs[0], outs[1]).astype(BF16))
        if b == 0:
            pool_linear()
        if b == 1:
            sgu_mix()
    kv_buf[pl.ds(0, ATT_BLOCK), :] = kv_buf[pl.ds(tm, ATT_BLOCK), :]
    while gate_chunks:
        gate_chunk()

    merged = gate_buf[0] * _dot(y_buf[0], w_br_ref[0])
    for n in range(1, N_BRANCH):
        merged = merged + gate_buf[n] * _dot(y_buf[n], w_br_ref[n])
    o_ref[...] = x + _dot(merged.astype(BF16), w_out_ref[...])


def _mixer(layer, x, ln_g, w_in, sinks, conv_w, pool_w, pool_scale, sgu_w, sgu_b_full, w_br, w_out):
    s = x.shape[0]
    tm = MIX_TM
    return pl.pallas_call(
        _mixer_body,
        grid=(s // tm,),
        in_specs=[
            pl.BlockSpec(memory_space=pltpu.SMEM),
            pl.BlockSpec((tm, D_MODEL), lambda i: (i, 0)),
            _const_spec(1, D_MODEL),
            _layer_spec(layer, D_MODEL, IN_COLS),
            _const_spec(3, BRANCH_WIDTH),
            _layer_spec(layer, 4, GROUP, GROUP),
            _const_spec(1, BRANCH_WIDTH),
            _layer_spec(layer, 4, GROUP, GROUP),
            _const_spec(4, GROUP, GROUP),
            _layer_spec(layer, N_BRANCH, BRANCH_WIDTH, D_MODEL),
            _layer_spec(layer, D_MODEL, D_MODEL),
        ],
        out_specs=pl.BlockSpec((tm, D_MODEL), lambda i: (i, 0)),
        out_shape=jax.ShapeDtypeStruct((s, D_MODEL), F32),
        scratch_shapes=[
            pltpu.VMEM((tm + ATT_BLOCK, 2 * LANES), F32),
            pltpu.VMEM((tm + CONV_HALO, BRANCH_WIDTH), F32),
            pltpu.VMEM((tm + POOL_HALO, BRANCH_WIDTH), F32),
            pltpu.VMEM((N_BRANCH, tm, BRANCH_WIDTH), BF16),
            pltpu.VMEM((N_BRANCH, tm, D_MODEL), F32),
        ],
        compiler_params=pltpu.CompilerParams(
            dimension_semantics=("arbitrary",), vmem_limit_bytes=VMEM_LIMIT_BYTES),
        name="mixer",
    )(sinks, x, ln_g, w_in, conv_w, pool_w, pool_scale, sgu_w, sgu_b_full, w_br, w_out)


def _swiglu_into(hb, wg_ref, wu_ref, wd_ref, acc_ref, base):
    for c in range(FFN_HIDDEN // FFN_FC):
        cs = slice(c * FFN_FC, (c + 1) * FFN_FC)
        g = _dot(hb, wg_ref[:, cs])
        u = _dot(hb, wu_ref[:, cs])
        a = (g * _sigmoid(g) * u).astype(BF16)
        d = _dot(a, wd_ref[cs, :])
        if c == 0:
            acc_ref[...] = d if base is None else base + d
        else:
            acc_ref[...] += d


def _dense_ffn_body(x_ref, g_ref, wg_ref, wu_ref, wd_ref, *rest):
    n_cast = (len(rest) - 1) // 2
    cast_in, o_ref, cast_out = rest[:n_cast], rest[n_cast], rest[n_cast + 1:]
    for src, dst in zip(cast_in, cast_out):
        dst[...] = src[...].astype(BF16)
    x = x_ref[...]
    hb = _rmsnorm(x, g_ref[...]).astype(BF16)
    _swiglu_into(hb, wg_ref, wu_ref, wd_ref, o_ref, x)


def _dense_ffn(layer, x, ln_g, wg, wu, wd, cast_layer, cast_weights):
    s = x.shape[0]
    tm = FFN_TM
    steps = s // tm
    cast_in_specs, cast_out_specs, cast_out_shapes = [], [], []
    for w in cast_weights:
        _, rows, cols = w.shape
        assert rows % (steps * 2 * SUBLANES) == 0
        cast_in_specs.append(pl.BlockSpec((None, rows // steps, cols), lambda i: (cast_layer, i, 0)))
        cast_out_specs.append(pl.BlockSpec((rows // steps, cols), lambda i: (i, 0)))
        cast_out_shapes.append(jax.ShapeDtypeStruct((rows, cols), BF16))
    out = pl.pallas_call(
        _dense_ffn_body,
        grid=(steps,),
        in_specs=[
            pl.BlockSpec((tm, D_MODEL), lambda i: (i, 0)),
            _const_spec(1, D_MODEL),
            _layer_spec(layer, D_MODEL, FFN_HIDDEN),
            _layer_spec(layer, D_MODEL, FFN_HIDDEN),
            _layer_spec(layer, FFN_HIDDEN, D_MODEL),
        ] + cast_in_specs,
        out_specs=[pl.BlockSpec((tm, D_MODEL), lambda i: (i, 0))] + cast_out_specs,
        out_shape=[jax.ShapeDtypeStruct((s, D_MODEL), F32)] + cast_out_shapes,
        compiler_params=pltpu.CompilerParams(
            dimension_semantics=("arbitrary",), vmem_limit_bytes=VMEM_LIMIT_BYTES),
        name="dense_ffn",
    )(x, ln_g, wg, wu, wd, *cast_weights)
    return out[0], out[1:]


def _expert_ffn_body(tile_expert_ref, n_used_ref, hs_ref, wg_ref, wu_ref, wd_ref, o_ref, acc_ref):
    del tile_expert_ref
    i = pl.program_id(0)
    tm = hs_ref.shape[0]

    @pl.when(i < n_used_ref[0])
    def _():
        hb = hs_ref[...].reshape(tm, D_MODEL).astype(BF16)
        _swiglu_into(hb, wg_ref, wu_ref, wd_ref, acc_ref, None)
        o_ref[...] = acc_ref[...].reshape(o_ref.shape)

    @pl.when(i >= n_used_ref[0])
    def _():
        o_ref[...] = jnp.zeros(o_ref.shape, F32)


def _expert_ffn(tile_expert, n_used, hs, wg, wu, wd):
    rows = hs.shape[0]
    tm = FFN_TM
    w_in_spec = pl.BlockSpec((None, D_MODEL, FFN_HIDDEN), lambda i, te, nu: (te[i], 0, 0))
    w_out_spec = pl.BlockSpec((None, FFN_HIDDEN, D_MODEL), lambda i, te, nu: (te[i], 0, 0))
    return pl.pallas_call(
        _expert_ffn_body,
        grid_spec=pltpu.PrefetchScalarGridSpec(
            num_scalar_prefetch=2,
            grid=(rows // tm,),
            in_specs=[
                pl.BlockSpec((tm,) + TOKEN_TILE, lambda i, te, nu: (jnp.minimum(i, nu[0] - 1), 0, 0)),
                w_in_spec, w_in_spec, w_out_spec,
            ],
            out_specs=pl.BlockSpec((tm,) + TOKEN_TILE, lambda i, te, nu: (i, 0, 0)),
            scratch_shapes=[pltpu.VMEM((tm, D_MODEL), F32)],
        ),
        out_shape=jax.ShapeDtypeStruct((rows,) + TOKEN_TILE, F32),
        compiler_params=pltpu.CompilerParams(
            dimension_semantics=("arbitrary",), vmem_limit_bytes=VMEM_LIMIT_BYTES),
        name="expert_ffn",
    )(tile_expert, n_used, hs, wg, wu, wd)


def _router_body(x_ref, g_ref, rw_ref, rb_ref, h_ref, meta_ref, cnt_ref, carry):
    tm = x_ref.shape[0]
    i = pl.program_id(0)

    @pl.when(i == 0)
    def _():
        carry[...] = jnp.zeros(carry.shape, F32)

    h = _rmsnorm(x_ref[...], g_ref[...])
    h_ref[...] = h.reshape(h_ref.shape)
    h_hi = h.astype(BF16)
    h_lo = (h - h_hi.astype(F32)).astype(BF16)
    w = rw_ref[...]
    w_hi = w.astype(BF16)
    w_lo = (w - w_hi.astype(F32)).astype(BF16)
    logits = _dot(h_hi, w_hi) + (_dot(h_lo, w_hi) + _dot(h_hi, w_lo)) + rb_ref[...]
    lane = lax.broadcasted_iota(jnp.int32, (tm, LANES), 1)
    logits = jnp.where(lane < N_EXPERTS, logits, NEG_INF)
    m1 = jnp.max(logits, axis=-1, keepdims=True)
    i1 = jnp.min(jnp.where(logits == m1, lane, LANES), axis=-1, keepdims=True)
    rest = jnp.where(lane == i1, NEG_INF, logits)
    m2 = jnp.max(rest, axis=-1, keepdims=True)
    i2 = jnp.min(jnp.where(rest == m2, lane, LANES), axis=-1, keepdims=True)
    e = jnp.exp(m2 - m1)
    w1 = 1.0 / (1.0 + e)
    w2 = e / (1.0 + e)
    oh1, oh2 = lane == i1, lane == i2
    cnt = jnp.where(oh1 | oh2, 1.0, 0.0)
    below = (lax.broadcasted_iota(jnp.int32, (tm, tm), 0) > lax.broadcasted_iota(jnp.int32, (tm, tm), 1))
    prefix = _dot(jnp.where(below, 1.0, 0.0).astype(BF16), cnt.astype(BF16)) + carry[0:1, :]
    r1 = jnp.sum(jnp.where(oh1, prefix, 0.0), axis=-1, keepdims=True)
    r2 = jnp.sum(jnp.where(oh2, prefix, 0.0), axis=-1, keepdims=True)
    meta = jnp.zeros((tm, LANES), F32)
    for k, v in enumerate((i1.astype(F32), i2.astype(F32), w1, w2, r1, r2)):
        meta = jnp.where(lane == k, v, meta)
    meta_ref[...] = meta
    carry[...] += jnp.sum(cnt, axis=0, keepdims=True)
    cnt_ref[...] = carry[...]


def _router(x, ln_g, rw_pad, rb_pad):
    s = x.shape[0]
    tm = ROUTER_TM
    return pl.pallas_call(
        _router_body,
        grid=(s // tm,),
        in_specs=[
            pl.BlockSpec((tm, D_MODEL), lambda i: (i, 0)),
            _const_spec(1, D_MODEL),
            _const_spec(D_MODEL, LANES),
            _const_spec(1, LANES),
        ],
        out_specs=[
            pl.BlockSpec((tm,) + TOKEN_TILE, lambda i: (i, 0, 0)),
            pl.BlockSpec((tm, LANES), lambda i: (i, 0)),
            pl.BlockSpec((SUBLANES, LANES), lambda i: (0, 0)),
        ],
        out_shape=[
            jax.ShapeDtypeStruct((s,) + TOKEN_TILE, F32),
            jax.ShapeDtypeStruct((s, LANES), F32),
            jax.ShapeDtypeStruct((SUBLANES, LANES), F32),
        ],
        scratch_shapes=[pltpu.VMEM((SUBLANES, LANES), F32)],
        compiler_params=pltpu.CompilerParams(
            dimension_semantics=("arbitrary",), vmem_limit_bytes=VMEM_LIMIT_BYTES),
        name="router",
    )(x, ln_g, rw_pad, rb_pad)


def _token_copy(src_ref, src_row, dst_ref, dst_row, sem):
    return pltpu.make_async_copy(src_ref.at[src_row], dst_ref.at[dst_row], sem)


def _gather_body(dest_ref, fill_start_ref, fill_on_ref, h_ref, hs_ref, zero_buf, sem):
    i = pl.program_id(0)

    def fill_copy(k):
        start = pl.multiple_of(fill_start_ref[k], FFN_TM)
        return pltpu.make_async_copy(zero_buf, hs_ref.at[pl.ds(start, FFN_TM)], sem)

    @pl.when(i == 0)
    def _():
        zero_buf[...] = jnp.zeros(zero_buf.shape, F32)
        for k in range(2 * N_EXPERTS):
            @pl.when(fill_on_ref[k] > 0)
            def _():
                fill_copy(k).start()
        for k in range(2 * N_EXPERTS):
            @pl.when(fill_on_ref[k] > 0)
            def _():
                fill_copy(k).wait()

    def start(r, c):
        t = i * GATHER_T + r
        _token_copy(h_ref, r, hs_ref, dest_ref[2 * t], sem).start(priority=0)
        _token_copy(h_ref, r, hs_ref, dest_ref[2 * t + 1], sem).start(priority=1)
        return c

    lax.fori_loop(0, GATHER_T, start, 0, unroll=8)

    def wait(r, c):
        _token_copy(h_ref, 0, hs_ref, 0, sem).wait()
        return c

    lax.fori_loop(0, 2 * GATHER_T, wait, 0, unroll=8)


def _gather_rows(dest, fill_start, fill_on, h, rows):
    s = h.shape[0]
    return pl.pallas_call(
        _gather_body,
        grid_spec=pltpu.PrefetchScalarGridSpec(
            num_scalar_prefetch=3,
            grid=(s // GATHER_T,),
            in_specs=[pl.BlockSpec((GATHER_T,) + TOKEN_TILE, lambda i, *_: (i, 0, 0))],
            out_specs=pl.BlockSpec(memory_space=pl.ANY),
            scratch_shapes=[pltpu.VMEM((FFN_TM,) + TOKEN_TILE, F32), pltpu.SemaphoreType.DMA],
        ),
        out_shape=jax.ShapeDtypeStruct((rows,) + TOKEN_TILE, F32),
        compiler_params=pltpu.CompilerParams(dimension_semantics=("arbitrary",)),
        name="gather_rows",
    )(dest, fill_start, fill_on, h)


def _combine_body(dest_ref, x_ref, meta_ref, ys_ref, *rest, final_norm):
    if final_norm:
        g_ref, o_ref, y_buf, sems = rest
    else:
        o_ref, y_buf, sems = rest
    i = pl.program_id(0)
    slot = i % 2

    def issue(step, slot_):
        def start(r, c):
            t = step * COMBINE_T + r
            _token_copy(ys_ref, dest_ref[2 * t], y_buf.at[slot_, 0], r, sems.at[slot_]).start(priority=0)
            _token_copy(ys_ref, dest_ref[2 * t + 1], y_buf.at[slot_, 1], r, sems.at[slot_]).start(priority=1)
            return c

        lax.fori_loop(0, COMBINE_T, start, 0, unroll=8)

    @pl.when(i == 0)
    def _():
        issue(0, 0)

    @pl.when(i + 1 < pl.num_programs(0))
    def _():
        issue(i + 1, 1 - slot)

    def wait(r, c):
        _token_copy(ys_ref, 0, y_buf.at[slot, 0], 0, sems.at[slot]).wait()
        return c

    lax.fori_loop(0, 2 * COMBINE_T, wait, 0, unroll=8)

    meta = meta_ref[...]
    lane = lax.broadcasted_iota(jnp.int32, meta.shape, 1)
    w1 = jnp.sum(jnp.where(lane == 2, meta, 0.0), axis=-1, keepdims=True)
    w2 = jnp.sum(jnp.where(lane == 3, meta, 0.0), axis=-1, keepdims=True)
    y1 = y_buf[slot, 0].reshape(COMBINE_T, D_MODEL)
    y2 = y_buf[slot, 1].reshape(COMBINE_T, D_MODEL)
    y = x_ref[...] + (w1 * y1 + w2 * y2)
    o_ref[...] = _rmsnorm(y, g_ref[...]) if final_norm else y


def _combine(dest, x, meta, ys, final_g=None):
    s = x.shape[0]
    t = COMBINE_T
    in_specs = [
        pl.BlockSpec((t, D_MODEL), lambda i, d: (i, 0)),
        pl.BlockSpec((t, LANES), lambda i, d: (i, 0)),
        pl.BlockSpec(memory_space=pl.ANY),
    ]
    args = [dest, x, meta, ys]
    if final_g is not None:
        in_specs.append(pl.BlockSpec((1, D_MODEL), lambda i, d: (0, 0)))
        args.append(final_g)
    return pl.pallas_call(
        functools.partial(_combine_body, final_norm=final_g is not None),
        grid_spec=pltpu.PrefetchScalarGridSpec(
            num_scalar_prefetch=1,
            grid=(s // t,),
            in_specs=in_specs,
            out_specs=pl.BlockSpec((t, D_MODEL), lambda i, d: (i, 0)),
            scratch_shapes=[pltpu.VMEM((2, 2, t) + TOKEN_TILE, F32), pltpu.SemaphoreType.DMA((2,))],
        ),
        out_shape=jax.ShapeDtypeStruct((s, D_MODEL), F32),
        compiler_params=pltpu.CompilerParams(dimension_semantics=("arbitrary",)),
        name="combine",
    )(*args)


def _moe_ffn(x, ln_g, router_w, router_b, wg, wu, wd, final_g):
    s = x.shape[0]
    tm = FFN_TM
    rw_pad = jnp.zeros((D_MODEL, LANES), F32).at[:, :N_EXPERTS].set(router_w)
    rb_pad = jnp.zeros((1, LANES), F32).at[0, :N_EXPERTS].set(router_b)
    h, meta, counts = _router(x, ln_g, rw_pad, rb_pad)
    idx = meta[:, 0:2].astype(jnp.int32)
    rank = meta[:, 4:6].astype(jnp.int32)
    cnt = counts[0, :N_EXPERTS].astype(jnp.int32)
    padded = (cnt + tm - 1) // tm * tm
    ends = jnp.cumsum(padded)
    dest = ((ends - padded)[idx] + rank).reshape(-1)
    n_tiles = 2 * s // tm + N_EXPERTS
    tile_start = jnp.arange(n_tiles, dtype=jnp.int32) * tm
    tile_expert = jnp.minimum(
        jnp.sum(tile_start[:, None] >= ends[None, :], axis=1), N_EXPERTS - 1).astype(jnp.int32)
    n_used = (ends[-1:] // tm).astype(jnp.int32)
    spare = ends[-1] + jnp.arange(N_EXPERTS, dtype=jnp.int32) * tm
    fill_start = jnp.concatenate([ends - tm, spare]).astype(jnp.int32)
    fill_on = jnp.concatenate([padded > 0, spare < n_tiles * tm]).astype(jnp.int32)
    fill_start = jnp.where(fill_on > 0, fill_start, 0)
    hs = _gather_rows(dest, fill_start, fill_on, h, n_tiles * tm)
    ys = _expert_ffn(tile_expert, n_used, hs, wg, wu, wd)
    return _combine(dest, x, meta, ys, final_g)


def kernel(x, ln_mix_g, w_in, attn_sinks, conv_w, pool_w, pool_scale, sgu_w, sgu_b, w_branch, w_out,
           ln_ffn_g, dense_w_gate, dense_w_up, dense_w_down, router_w, router_b,
           moe_w_gate, moe_w_up, moe_w_down, ln_final_g):
    batch, seq, _ = x.shape
    depth = w_in.shape[0]
    assert batch == 1 and seq % MIX_TM == 0 and seq % FFN_TM == 0 and depth % 2 == 0
    assert D_MODEL == SUBLANES * LANES
    xs = x.reshape(seq, D_MODEL)
    w_in, pool_w, sgu_w, w_branch, w_out = (w.astype(BF16) for w in (w_in, pool_w, sgu_w, w_branch, w_out))
    dense_w = [w.astype(BF16) for w in (dense_w_gate, dense_w_up, dense_w_down)]
    n_moe = moe_w_gate.shape[0]
    moe_w_f32 = [w.reshape(n_moe, -1, w.shape[-1]) for w in (moe_w_gate, moe_w_up, moe_w_down)]
    for layer in range(depth):
        sgu_b_full = jnp.broadcast_to(sgu_b[layer][:, :, None], (4, GROUP, GROUP))
        xs = _mixer(layer, xs, ln_mix_g[layer][None], w_in, attn_sinks[layer], conv_w[layer], pool_w,
                    pool_scale[layer][None], sgu_w, sgu_b_full, w_branch, w_out)
        i = layer // 2
        if layer % 2 == 0:
            xs, moe_w = _dense_ffn(i, xs, ln_ffn_g[layer][None], *dense_w, i, moe_w_f32)
            moe_w = [w.reshape(N_EXPERTS, -1, w.shape[-1]) for w in moe_w]
        else:
            final_g = ln_final_g[None] if layer == depth - 1 else None
            xs = _moe_ffn(xs, ln_ffn_g[layer][None], router_w[i], router_b[i], *moe_w, final_g)
    return xs.reshape(batch, seq, D_MODEL)
```

```python
import functools

import jax
import jax.numpy as jnp
from jax import lax
from jax.experimental import pallas as pl
from jax.experimental.pallas import tpu as pltpu

F32 = jnp.float32
BF16 = jnp.bfloat16

D_MODEL = 1024
ATT_HEADS = 8
ATT_KV_HEADS = 2
HEAD_DIM = 64
ATT_BLOCK = 128
POOL_SIZES = (2, 4, 8, 16)
GROUP = 128
N_BRANCH = 4
BRANCH_WIDTH = 512
FFN_HIDDEN = 2816
N_EXPERTS = 8
RMS_EPS = 1e-6
NEG_INF = -1e30

_C_Q, _C_KV, _C_CONV, _C_POOL, _C_SGU, _C_GATE = 0, 512, 768, 2304, 2816, 3840
IN_COLS = _C_GATE + N_BRANCH * D_MODEL

LANES = 128
SUBLANES = 8
TOKEN_TILE = (SUBLANES, LANES)
MXU_COLS = 256
VMEM_LIMIT_BYTES = 60 * 1024 * 1024

MIX_TM = 512
FFN_TM = 512
FFN_FC = MXU_COLS
ROUTER_TM = 512
GATHER_T = 512
COMBINE_T = 256
CONV_HALO = SUBLANES
POOL_HALO = 2 * SUBLANES


def _rmsnorm(x, g):
    return x * lax.rsqrt(jnp.mean(x * x, axis=-1, keepdims=True) + RMS_EPS) * g


def _sigmoid(x):
    return 1.0 / (1.0 + jnp.exp(-x))


def _dot(a, b):
    return jnp.dot(a, b, preferred_element_type=F32)


def _const_spec(*shape):
    return pl.BlockSpec(shape, lambda *_: (0,) * len(shape), pipeline_mode=pl.Buffered(1))


def _layer_spec(layer, *shape):
    return pl.BlockSpec((None,) + shape, lambda *_: (layer,) + (0,) * len(shape),
                        pipeline_mode=pl.Buffered(1))


def _mixer_body(sinks_ref, x_ref, g_ref, w_in_ref, conv_w_ref, pool_w_ref, pool_scale_ref,
                sgu_w_ref, sgu_b_ref, w_br_ref, w_out_ref, o_ref,
                kv_buf, z_buf, p_buf, y_buf, gate_buf):
    tm = x_ref.shape[0]
    nblk = tm // ATT_BLOCK
    i = pl.program_id(0)

    @pl.when(i == 0)
    def _():
        kv_buf[pl.ds(0, ATT_BLOCK), :] = jnp.zeros((ATT_BLOCK, kv_buf.shape[1]), F32)
        z_buf[pl.ds(0, CONV_HALO), :] = jnp.zeros((CONV_HALO, z_buf.shape[1]), F32)
        p_buf[pl.ds(0, POOL_HALO), :] = jnp.zeros((POOL_HALO, p_buf.shape[1]), F32)

    x = x_ref[...]
    hb = _rmsnorm(x, g_ref[...]).astype(BF16)

    def proj(c0, n):
        return _dot(hb, w_in_ref[:, c0:c0 + n])

    gate_chunks = [(n, c) for n in range(N_BRANCH) for c in range(D_MODEL // MXU_COLS)]

    def gate_chunk():
        n, c = gate_chunks.pop(0)
        cols = slice(c * MXU_COLS, (c + 1) * MXU_COLS)
        gate_buf[n, :, cols] = _sigmoid(proj(_C_GATE + n * D_MODEL + c * MXU_COLS, MXU_COLS))

    qkv = proj(_C_Q, 768)
    kv_buf[pl.ds(ATT_BLOCK, tm), :] = qkv[:, 512:768]
    q = qkv[:, 0:512] * (HEAD_DIM ** -0.5)
    c3 = proj(_C_CONV, 3 * BRANCH_WIDTH)
    pp = proj(_C_POOL, BRANCH_WIDTH)
    usv = proj(_C_SGU, 2 * BRANCH_WIDTH)

    z = c3[:, 512:1024] * c3[:, 1024:1536]
    z_buf[pl.ds(CONV_HALO, tm), :] = z
    cw = conv_w_ref[...]
    conv = (cw[0:1] * z_buf[pl.ds(CONV_HALO - 2, tm), :] + cw[1:2] * z_buf[pl.ds(CONV_HALO - 1, tm), :]
            + cw[2:3] * z)
    z_buf[pl.ds(0, CONV_HALO), :] = z_buf[pl.ds(tm, CONV_HALO), :]
    y_buf[1] = (c3[:, 0:512] * conv).astype(BF16)

    p_buf[pl.ds(POOL_HALO, tm), :] = pp
    tpos = i * tm + lax.broadcasted_iota(jnp.int32, (tm, GROUP), 0)
    pooled = []
    for gi, w in enumerate(POOL_SIZES):
        cs = slice(gi * GROUP, (gi + 1) * GROUP)
        win = pp[:, cs]
        for k in range(1, w):
            win = win + p_buf[pl.ds(POOL_HALO - k, tm), cs]
        cnt = jnp.minimum(tpos + 1, w).astype(F32)
        pooled.append((win / cnt - pp[:, cs]).astype(BF16))
    p_buf[pl.ds(0, POOL_HALO), :] = p_buf[pl.ds(tm, POOL_HALO), :]

    def pool_linear():
        for gi in range(len(POOL_SIZES)):
            cs = slice(gi * GROUP, (gi + 1) * GROUP)
            y_buf[2, :, cs] = (_dot(pooled[gi], pool_w_ref[gi]) * pool_scale_ref[:, cs]).astype(BF16)

    def sgu_mix():
        tril = (lax.broadcasted_iota(jnp.int32, (GROUP, GROUP), 0)
                >= lax.broadcasted_iota(jnp.int32, (GROUP, GROUP), 1))
        for g in range(BRANCH_WIDTH // GROUP):
            wg = jnp.where(tril, sgu_w_ref[g], jnp.zeros((GROUP, GROUP), BF16))
            for c in range(nblk):
                rows = slice(c * GROUP, (c + 1) * GROUP)
                sv = usv[rows, BRANCH_WIDTH + g * GROUP:BRANCH_WIDTH + (g + 1) * GROUP].astype(BF16)
                mixed = _dot(wg, sv) + sgu_b_ref[g]
                y_buf[3, pl.ds(c * GROUP, GROUP), g * GROUP:(g + 1) * GROUP] = (
                    usv[rows, g * GROUP:(g + 1) * GROUP] * mixed).astype(BF16)

    lo128 = lax.broadcasted_iota(jnp.int32, (ATT_BLOCK, LANES), 1) < HEAD_DIM
    lo256 = lax.broadcasted_iota(jnp.int32, (2 * ATT_BLOCK, LANES), 1) < HEAD_DIM
    qi = lax.broadcasted_iota(jnp.int32, (ATT_BLOCK, 2 * ATT_BLOCK), 0)
    kj = lax.broadcasted_iota(jnp.int32, (ATT_BLOCK, 2 * ATT_BLOCK), 1)
    rel = qi + ATT_BLOCK - kj
    local = (rel >= 0) & (rel < ATT_BLOCK)

    def attention_operands(b):
        kvb = kv_buf[pl.ds(b * ATT_BLOCK, 2 * ATT_BLOCK), :]
        kb, vb = kvb[:, 0:LANES], kvb[:, LANES:2 * LANES]
        kr, vr = pltpu.roll(kb, HEAD_DIM, 1), pltpu.roll(vb, HEAD_DIM, 1)
        kkT = [jnp.where(lo256, kb, kr).T.astype(BF16), jnp.where(lo256, kr, kb).T.astype(BF16)]
        vv = [jnp.where(lo256, vb, vr).astype(BF16), jnp.where(lo256, vr, vb).astype(BF16)]
        valid = local & (kj + (i * tm + (b - 1) * ATT_BLOCK) >= 0)
        return kkT, vv, valid

    pairs = [(b, j) for b in range(nblk) for j in range(ATT_HEADS // 2)]
    block_ops = {0: attention_operands(0)}

    def pair_scores(b, j):
        kkT, _, valid = block_ops[b]
        qp = q[b * ATT_BLOCK:(b + 1) * ATT_BLOCK, j * LANES:(j + 1) * LANES]
        kvh = (2 * j) // (ATT_HEADS // ATT_KV_HEADS)
        out = []
        for half in range(2):
            keep = lo128 if half == 0 else jnp.logical_not(lo128)
            qm = jnp.where(keep, qp, 0.0).astype(BF16)
            out.append(jnp.where(valid, _dot(qm, kkT[kvh]), NEG_INF))
        return out

    scores = pair_scores(*pairs[0])
    for k, (b, j) in enumerate(pairs):
        if j == 1 and b + 1 < nblk:
            block_ops[b + 1] = attention_operands(b + 1)
        nxt = pair_scores(*pairs[k + 1]) if k + 1 < len(pairs) else None
        gate_chunk()
        vv = block_ops[b][1]
        kvh = (2 * j) // (ATT_HEADS // ATT_KV_HEADS)
        outs = []
        for half in range(2):
            s = scores[half]
            sink = sinks_ref[2 * j + half]
            m = jnp.maximum(jnp.max(s, axis=-1, keepdims=True), sink)
            p = jnp.exp(s - m)
            den = jnp.sum(p, axis=-1, keepdims=True) + jnp.exp(sink - m)
            outs.append(_dot(p.astype(BF16), vv[kvh]) / den)
        y_buf[0, pl.ds(b * ATT_BLOCK, ATT_BLOCK), j * LANES:(j + 1) * LANES] = (
            jnp.where(lo128, outs[0], outs[1]).astype(BF16))
        scores = nxt
        if (b, j) == (0, ATT_HEADS // 2 - 1):
            pool_linear()
        if (b, j) == (1, ATT_HEADS // 2 - 1):
            sgu_mix()
    kv_buf[pl.ds(0, ATT_BLOCK), :] = kv_buf[pl.ds(tm, ATT_BLOCK), :]
    while gate_chunks:
        gate_chunk()

    merged = gate_buf[0] * _dot(y_buf[0], w_br_ref[0])
    for n in range(1, N_BRANCH):
        merged = merged + gate_buf[n] * _dot(y_buf[n], w_br_ref[n])
    o_ref[...] = x + _dot(merged.astype(BF16), w_out_ref[...])


def _mixer(layer, x, ln_g, w_in, sinks, conv_w, pool_w, pool_scale, sgu_w, sgu_b_full, w_br, w_out):
    s = x.shape[0]
    tm = MIX_TM
    return pl.pallas_call(
        _mixer_body,
        grid=(s // tm,),
        in_specs=[
            pl.BlockSpec(memory_space=pltpu.SMEM),
            pl.BlockSpec((tm, D_MODEL), lambda i: (i, 0)),
            _const_spec(1, D_MODEL),
            _layer_spec(layer, D_MODEL, IN_COLS),
            _const_spec(3, BRANCH_WIDTH),
            _layer_spec(layer, 4, GROUP, GROUP),
            _const_spec(1, BRANCH_WIDTH),
            _layer_spec(layer, 4, GROUP, GROUP),
            _const_spec(4, GROUP, GROUP),
            _layer_spec(layer, N_BRANCH, BRANCH_WIDTH, D_MODEL),
            _layer_spec(layer, D_MODEL, D_MODEL),
        ],
        out_specs=pl.BlockSpec((tm, D_MODEL), lambda i: (i, 0)),
        out_shape=jax.ShapeDtypeStruct((s, D_MODEL), F32),
        scratch_shapes=[
            pltpu.VMEM((tm + ATT_BLOCK, 2 * LANES), F32),
            pltpu.VMEM((tm + CONV_HALO, BRANCH_WIDTH), F32),
            pltpu.VMEM((tm + POOL_HALO, BRANCH_WIDTH), F32),
            pltpu.VMEM((N_BRANCH, tm, BRANCH_WIDTH), BF16),
            pltpu.VMEM((N_BRANCH, tm, D_MODEL), F32),
        ],
        compiler_params=pltpu.CompilerParams(
            dimension_semantics=("arbitrary",), vmem_limit_bytes=VMEM_LIMIT_BYTES),
        name="mixer",
    )(sinks, x, ln_g, w_in, conv_w, pool_w, pool_scale, sgu_w, sgu_b_full, w_br, w_out)


def _swiglu_into(hb, wg_ref, wu_ref, wd_ref, acc_ref, base):
    for c in range(FFN_HIDDEN // FFN_FC):
        cs = slice(c * FFN_FC, (c + 1) * FFN_FC)
        g = _dot(hb, wg_ref[:, cs])
        u = _dot(hb, wu_ref[:, cs])
        a = (g * _sigmoid(g) * u).astype(BF16)
        d = _dot(a, wd_ref[cs, :])
        if c == 0:
            acc_ref[...] = d if base is None else base + d
        else:
            acc_ref[...] += d


def _dense_ffn_body(x_ref, g_ref, wg_ref, wu_ref, wd_ref, *rest):
    n_cast = (len(rest) - 1) // 2
    cast_in, o_ref, cast_out = rest[:n_cast], rest[n_cast], rest[n_cast + 1:]
    for src, dst in zip(cast_in, cast_out):
        dst[...] = src[...].astype(BF16)
    x = x_ref[...]
    hb = _rmsnorm(x, g_ref[...]).astype(BF16)
    _swiglu_into(hb, wg_ref, wu_ref, wd_ref, o_ref, x)


def _dense_ffn(layer, x, ln_g, wg, wu, wd, cast_layer, cast_weights):
    s = x.shape[0]
    tm = FFN_TM
    steps = s // tm
    cast_in_specs, cast_out_specs, cast_out_shapes = [], [], []
    for w in cast_weights:
        _, rows, cols = w.shape
        assert rows % (steps * 2 * SUBLANES) == 0
        cast_in_specs.append(pl.BlockSpec((None, rows // steps, cols), lambda i: (cast_layer, i, 0)))
        cast_out_specs.append(pl.BlockSpec((rows // steps, cols), lambda i: (i, 0)))
        cast_out_shapes.append(jax.ShapeDtypeStruct((rows, cols), BF16))
    out = pl.pallas_call(
        _dense_ffn_body,
        grid=(steps,),
        in_specs=[
            pl.BlockSpec((tm, D_MODEL), lambda i: (i, 0)),
            _const_spec(1, D_MODEL),
            _layer_spec(layer, D_MODEL, FFN_HIDDEN),
            _layer_spec(layer, D_MODEL, FFN_HIDDEN),
            _layer_spec(layer, FFN_HIDDEN, D_MODEL),
        ] + cast_in_specs,
        out_specs=[pl.BlockSpec((tm, D_MODEL), lambda i: (i, 0))] + cast_out_specs,
        out_shape=[jax.ShapeDtypeStruct((s, D_MODEL), F32)] + cast_out_shapes,
        compiler_params=pltpu.CompilerParams(
            dimension_semantics=("arbitrary",), vmem_limit_bytes=VMEM_LIMIT_BYTES),
        name="dense_ffn",
    )(x, ln_g, wg, wu, wd, *cast_weights)
    return out[0], out[1:]


def _expert_ffn_body(tile_expert_ref, n_used_ref, hs_ref, wg_ref, wu_ref, wd_ref, o_ref, acc_ref):
    del tile_expert_ref
    i = pl.program_id(0)
    tm = hs_ref.shape[0]

    @pl.when(i < n_used_ref[0])
    def _():
        hb = hs_ref[...].reshape(tm, D_MODEL).astype(BF16)
        _swiglu_into(hb, wg_ref, wu_ref, wd_ref, acc_ref, None)
        o_ref[...] = acc_ref[...].reshape(o_ref.shape)

    @pl.when(i >= n_used_ref[0])
    def _():
        o_ref[...] = jnp.zeros(o_ref.shape, F32)


def _expert_ffn(tile_expert, n_used, hs, wg, wu, wd):
    rows = hs.shape[0]
    tm = FFN_TM
    w_in_spec = pl.BlockSpec((None, D_MODEL, FFN_HIDDEN), lambda i, te, nu: (te[i], 0, 0))
    w_out_spec = pl.BlockSpec((None, FFN_HIDDEN, D_MODEL), lambda i, te, nu: (te[i], 0, 0))
    return pl.pallas_call(
        _expert_ffn_body,
        grid_spec=pltpu.PrefetchScalarGridSpec(
            num_scalar_prefetch=2,
            grid=(rows // tm,),
            in_specs=[
                pl.BlockSpec((tm,) + TOKEN_TILE, lambda i, te, nu: (jnp.minimum(i, nu[0] - 1), 0, 0)),
                w_in_spec, w_in_spec, w_out_spec,
            ],
            out_specs=pl.BlockSpec((tm,) + TOKEN_TILE, lambda i, te, nu: (i, 0, 0)),
            scratch_shapes=[pltpu.VMEM((tm, D_MODEL), F32)],
        ),
        out_shape=jax.ShapeDtypeStruct((rows,) + TOKEN_TILE, F32),
        compiler_params=pltpu.CompilerParams(
            dimension_semantics=("arbitrary",), vmem_limit_bytes=VMEM_LIMIT_BYTES),
        name="expert_ffn",
    )(tile_expert, n_used, hs, wg, wu, wd)


def _router_body(x_ref, g_ref, rw_ref, rb_ref, h_ref, meta_ref, cnt_ref, carry):
    tm = x_ref.shape[0]
    i = pl.program_id(0)

    @pl.when(i == 0)
    def _():
        carry[...] = jnp.zeros(carry.shape, F32)

    h = _rmsnorm(x_ref[...], g_ref[...])
    h_ref[...] = h.reshape(h_ref.shape)
    h_hi = h.astype(BF16)
    h_lo = (h - h_hi.astype(F32)).astype(BF16)
    w = rw_ref[...]
    w_hi = w.astype(BF16)
    w_lo = (w - w_hi.astype(F32)).astype(BF16)
    logits = _dot(h_hi, w_hi) + (_dot(h_lo, w_hi) + _dot(h_hi, w_lo)) + rb_ref[...]
    lane = lax.broadcasted_iota(jnp.int32, (tm, LANES), 1)
    logits = jnp.where(lane < N_EXPERTS, logits, NEG_INF)
    m1 = jnp.max(logits, axis=-1, keepdims=True)
    i1 = jnp.min(jnp.where(logits == m1, lane, LANES), axis=-1, keepdims=True)
    rest = jnp.where(lane == i1, NEG_INF, logits)
    m2 = jnp.max(rest, axis=-1, keepdims=True)
    i2 = jnp.min(jnp.where(rest == m2, lane, LANES), axis=-1, keepdims=True)
    e = jnp.exp(m2 - m1)
    w1 = 1.0 / (1.0 + e)
    w2 = e / (1.0 + e)
    oh1, oh2 = lane == i1, lane == i2
    cnt = jnp.where(oh1 | oh2, 1.0, 0.0)
    below = (lax.broadcasted_iota(jnp.int32, (tm, tm), 0) > lax.broadcasted_iota(jnp.int32, (tm, tm), 1))
    prefix = _dot(jnp.where(below, 1.0, 0.0).astype(BF16), cnt.astype(BF16)) + carry[0:1, :]
    r1 = jnp.sum(jnp.where(oh1, prefix, 0.0), axis=-1, keepdims=True)
    r2 = jnp.sum(jnp.where(oh2, prefix, 0.0), axis=-1, keepdims=True)
    meta = jnp.zeros((tm, LANES), F32)
    for k, v in enumerate((i1.astype(F32), i2.astype(F32), w1, w2, r1, r2)):
        meta = jnp.where(lane == k, v, meta)
    meta_ref[...] = meta
    carry[...] += jnp.sum(cnt, axis=0, keepdims=True)
    cnt_ref[...] = carry[...]


def _router(x, ln_g, rw_pad, rb_pad):
    s = x.shape[0]
    tm = ROUTER_TM
    return pl.pallas_call(
        _router_body,
        grid=(s // tm,),
        in_specs=[
            pl.BlockSpec((tm, D_MODEL), lambda i: (i, 0)),
            _const_spec(1, D_MODEL),
            _const_spec(D_MODEL, LANES),
            _const_spec(1, LANES),
        ],
        out_specs=[
            pl.BlockSpec((tm,) + TOKEN_TILE, lambda i: (i, 0, 0)),
            pl.BlockSpec((tm, LANES), lambda i: (i, 0)),
            pl.BlockSpec((SUBLANES, LANES), lambda i: (0, 0)),
        ],
        out_shape=[
            jax.ShapeDtypeStruct((s,) + TOKEN_TILE, F32),
            jax.ShapeDtypeStruct((s, LANES), F32),
            jax.ShapeDtypeStruct((SUBLANES, LANES), F32),
        ],
        scratch_shapes=[pltpu.VMEM((SUBLANES, LANES), F32)],
        compiler_params=pltpu.CompilerParams(
            dimension_semantics=("arbitrary",), vmem_limit_bytes=VMEM_LIMIT_BYTES),
        name="router",
    )(x, ln_g, rw_pad, rb_pad)


def _token_copy(src_ref, src_row, dst_ref, dst_row, sem):
    return pltpu.make_async_copy(src_ref.at[src_row], dst_ref.at[dst_row], sem)


def _gather_body(dest_ref, fill_start_ref, fill_on_ref, h_ref, hs_ref, zero_buf, sem):
    i = pl.program_id(0)

    def fill_copy(k):
        start = pl.multiple_of(fill_start_ref[k], FFN_TM)
        return pltpu.make_async_copy(zero_buf, hs_ref.at[pl.ds(start, FFN_TM)], sem)

    @pl.when(i == 0)
    def _():
        zero_buf[...] = jnp.zeros(zero_buf.shape, F32)
        for k in range(2 * N_EXPERTS):
            @pl.when(fill_on_ref[k] > 0)
            def _():
                fill_copy(k).start()
        for k in range(2 * N_EXPERTS):
            @pl.when(fill_on_ref[k] > 0)
            def _():
                fill_copy(k).wait()

    def start(r, c):
        t = i * GATHER_T + r
        _token_copy(h_ref, r, hs_ref, dest_ref[2 * t], sem).start(priority=0)
        _token_copy(h_ref, r, hs_ref, dest_ref[2 * t + 1], sem).start(priority=1)
        return c

    lax.fori_loop(0, GATHER_T, start, 0, unroll=8)

    def wait(r, c):
        _token_copy(h_ref, 0, hs_ref, 0, sem).wait()
        return c

    lax.fori_loop(0, 2 * GATHER_T, wait, 0, unroll=8)


def _gather_rows(dest, fill_start, fill_on, h, rows):
    s = h.shape[0]
    return pl.pallas_call(
        _gather_body,
        grid_spec=pltpu.PrefetchScalarGridSpec(
            num_scalar_prefetch=3,
            grid=(s // GATHER_T,),
            in_specs=[pl.BlockSpec((GATHER_T,) + TOKEN_TILE, lambda i, *_: (i, 0, 0))],
            out_specs=pl.BlockSpec(memory_space=pl.ANY),
            scratch_shapes=[pltpu.VMEM((FFN_TM,) + TOKEN_TILE, F32), pltpu.SemaphoreType.DMA],
        ),
        out_shape=jax.ShapeDtypeStruct((rows,) + TOKEN_TILE, F32),
        compiler_params=pltpu.CompilerParams(dimension_semantics=("arbitrary",)),
        name="gather_rows",
    )(dest, fill_start, fill_on, h)


def _combine_body(dest_ref, x_ref, meta_ref, ys_ref, *rest, final_norm):
    if final_norm:
        g_ref, o_ref, y_buf, sems = rest
    else:
        o_ref, y_buf, sems = rest
    i = pl.program_id(0)
    slot = i % 2

    def issue(step, slot_):
        def start(r, c):
            t = step * COMBINE_T + r
            _token_copy(ys_ref, dest_ref[2 * t], y_buf.at[slot_, 0], r, sems.at[slot_]).start(priority=0)
            _token_copy(ys_ref, dest_ref[2 * t + 1], y_buf.at[slot_, 1], r, sems.at[slot_]).start(priority=1)
            return c

        lax.fori_loop(0, COMBINE_T, start, 0, unroll=8)

    @pl.when(i == 0)
    def _():
        issue(0, 0)

    @pl.when(i + 1 < pl.num_programs(0))
    def _():
        issue(i + 1, 1 - slot)

    def wait(r, c):
        _token_copy(ys_ref, 0, y_buf.at[slot, 0], 0, sems.at[slot]).wait()
        return c

    lax.fori_loop(0, 2 * COMBINE_T, wait, 0, unroll=8)

    meta = meta_ref[...]
    lane = lax.broadcasted_iota(jnp.int32, meta.shape, 1)
    w1 = jnp.sum(jnp.where(lane == 2, meta, 0.0), axis=-1, keepdims=True)
    w2 = jnp.sum(jnp.where(lane == 3, meta, 0.0), axis=-1, keepdims=True)
    y1 = y_buf[slot, 0].reshape(COMBINE_T, D_MODEL)
    y2 = y_buf[slot, 1].reshape(COMBINE_T, D_MODEL)
    y = x_ref[...] + (w1 * y1 + w2 * y2)
    o_ref[...] = _rmsnorm(y, g_ref[...]) if final_norm else y


def _combine(dest, x, meta, ys, final_g=None):
    s = x.shape[0]
    t = COMBINE_T
    in_specs = [
        pl.BlockSpec((t, D_MODEL), lambda i, d: (i, 0)),
        pl.BlockSpec((t, LANES), lambda i, d: (i, 0)),
        pl.BlockSpec(memory_space=pl.ANY),
    ]
    args = [dest, x, meta, ys]
    if final_g is not None:
        in_specs.append(pl.BlockSpec((1, D_MODEL), lambda i, d: (0, 0)))
        args.append(final_g)
    return pl.pallas_call(
        functools.partial(_combine_body, final_norm=final_g is not None),
        grid_spec=pltpu.PrefetchScalarGridSpec(
            num_scalar_prefetch=1,
            grid=(s // t,),
            in_specs=in_specs,
            out_specs=pl.BlockSpec((t, D_MODEL), lambda i, d: (i, 0)),
            scratch_shapes=[pltpu.VMEM((2, 2, t) + TOKEN_TILE, F32), pltpu.SemaphoreType.DMA((2,))],
        ),
        out_shape=jax.ShapeDtypeStruct((s, D_MODEL), F32),
        compiler_params=pltpu.CompilerParams(dimension_semantics=("arbitrary",)),
        name="combine",
    )(*args)


def _moe_ffn(x, ln_g, router_w, router_b, wg, wu, wd, final_g):
    s = x.shape[0]
    tm = FFN_TM
    rw_pad = jnp.zeros((D_MODEL, LANES), F32).at[:, :N_EXPERTS].set(router_w)
    rb_pad = jnp.zeros((1, LANES), F32).at[0, :N_EXPERTS].set(router_b)
    h, meta, counts = _router(x, ln_g, rw_pad, rb_pad)
    idx = meta[:, 0:2].astype(jnp.int32)
    rank = meta[:, 4:6].astype(jnp.int32)
    cnt = counts[0, :N_EXPERTS].astype(jnp.int32)
    padded = (cnt + tm - 1) // tm * tm
    ends = jnp.cumsum(padded)
    dest = ((ends - padded)[idx] + rank).reshape(-1)
    n_tiles = 2 * s // tm + N_EXPERTS
    tile_start = jnp.arange(n_tiles, dtype=jnp.int32) * tm
    tile_expert = jnp.minimum(
        jnp.sum(tile_start[:, None] >= ends[None, :], axis=1), N_EXPERTS - 1).astype(jnp.int32)
    n_used = (ends[-1:] // tm).astype(jnp.int32)
    spare = ends[-1] + jnp.arange(N_EXPERTS, dtype=jnp.int32) * tm
    fill_start = jnp.concatenate([ends - tm, spare]).astype(jnp.int32)
    fill_on = jnp.concatenate([padded > 0, spare < n_tiles * tm]).astype(jnp.int32)
    fill_start = jnp.where(fill_on > 0, fill_start, 0)
    hs = _gather_rows(dest, fill_start, fill_on, h, n_tiles * tm)
    ys = _expert_ffn(tile_expert, n_used, hs, wg, wu, wd)
    return _combine(dest, x, meta, ys, final_g)


def kernel(x, ln_mix_g, w_in, attn_sinks, conv_w, pool_w, pool_scale, sgu_w, sgu_b, w_branch, w_out,
           ln_ffn_g, dense_w_gate, dense_w_up, dense_w_down, router_w, router_b,
           moe_w_gate, moe_w_up, moe_w_down, ln_final_g):
    batch, seq, _ = x.shape
    depth = w_in.shape[0]
    assert batch == 1 and seq % MIX_TM == 0 and seq % FFN_TM == 0 and depth % 2 == 0
    assert D_MODEL == SUBLANES * LANES
    xs = x.reshape(seq, D_MODEL)
    w_in, pool_w, sgu_w, w_branch, w_out = (w.astype(BF16) for w in (w_in, pool_w, sgu_w, w_branch, w_out))
    dense_w = [w.astype(BF16) for w in (dense_w_gate, dense_w_up, dense_w_down)]
    n_moe = moe_w_gate.shape[0]
    moe_w_f32 = [w.reshape(n_moe, -1, w.shape[-1]) for w in (moe_w_gate, moe_w_up, moe_w_down)]
    for layer in range(depth):
        sgu_b_full = jnp.broadcast_to(sgu_b[layer][:, :, None], (4, GROUP, GROUP))
        xs = _mixer(layer, xs, ln_mix_g[layer][None], w_in, attn_sinks[layer], conv_w[layer], pool_w,
                    pool_scale[layer][None], sgu_w, sgu_b_full, w_branch, w_out)
        i = layer // 2
        if layer % 2 == 0:
            xs, moe_w = _dense_ffn(i, xs, ln_ffn_g[layer][None], *dense_w, i, moe_w_f32)
            moe_w = [w.reshape(N_EXPERTS, -1, w.shape[-1]) for w in moe_w]
        else:
            final_g = ln_final_g[None] if layer == depth - 1 else None
            xs = _moe_ffn(xs, ln_ffn_g[layer][None], router_w[i], router_b[i], *moe_w, final_g)
    return xs.reshape(batch, seq, D_MODEL)
```

```python
import functools

import jax
import jax.numpy as jnp
from jax import lax
from jax.experimental import pallas as pl
from jax.experimental.pallas import tpu as pltpu

F32 = jnp.float32
BF16 = jnp.bfloat16

D_MODEL = 1024
ATT_HEADS = 8
ATT_KV_HEADS = 2
HEAD_DIM = 64
ATT_BLOCK = 128
POOL_SIZES = (2, 4, 8, 16)
GROUP = 128
N_BRANCH = 4
BRANCH_WIDTH = 512
FFN_HIDDEN = 2816
N_EXPERTS = 8
RMS_EPS = 1e-6
NEG_INF = -1e30

_C_Q, _C_KV, _C_CONV, _C_POOL, _C_SGU, _C_GATE = 0, 512, 768, 2304, 2816, 3840
IN_COLS = _C_GATE + N_BRANCH * D_MODEL

LANES = 128
SUBLANES = 8
TOKEN_TILE = (SUBLANES, LANES)
MXU_COLS = 256
VMEM_LIMIT_BYTES = 60 * 1024 * 1024

MIX_TM = 512
FFN_TM = 512
FFN_FC = MXU_COLS
ROUTER_TM = 1024
ROUTER_SUB = 512
GATHER_T = 512
COMBINE_T = 512
CONV_HALO = SUBLANES
POOL_HALO = 2 * SUBLANES


def _rmsnorm(x, g):
    return x * lax.rsqrt(jnp.mean(x * x, axis=-1, keepdims=True) + RMS_EPS) * g


def _sigmoid(x):
    return 1.0 / (1.0 + jnp.exp(-x))


def _dot(a, b):
    return jnp.dot(a, b, preferred_element_type=F32)


def _const_spec(*shape):
    return pl.BlockSpec(shape, lambda *_: (0,) * len(shape), pipeline_mode=pl.Buffered(1))


def _layer_spec(layer, *shape):
    return pl.BlockSpec((None,) + shape, lambda *_: (layer,) + (0,) * len(shape),
                        pipeline_mode=pl.Buffered(1))


def _mixer_body(sinks_ref, x_ref, g_ref, w_in_ref, conv_w_ref, pool_w_ref, pool_scale_ref,
                sgu_w_ref, sgu_b_ref, w_br_ref, w_out_ref, o_ref,
                kv_buf, z_buf, p_buf, y_buf, gate_buf):
    tm = x_ref.shape[0]
    nblk = tm // ATT_BLOCK
    i = pl.program_id(0)

    @pl.when(i == 0)
    def _():
        kv_buf[pl.ds(0, ATT_BLOCK), :] = jnp.zeros((ATT_BLOCK, kv_buf.shape[1]), F32)
        z_buf[pl.ds(0, CONV_HALO), :] = jnp.zeros((CONV_HALO, z_buf.shape[1]), F32)
        p_buf[pl.ds(0, POOL_HALO), :] = jnp.zeros((POOL_HALO, p_buf.shape[1]), F32)

    x = x_ref[...]
    hb = _rmsnorm(x, g_ref[...]).astype(BF16)

    def proj(c0, n):
        return _dot(hb, w_in_ref[:, c0:c0 + n])

    gate_chunks = [(n, c) for n in range(N_BRANCH) for c in range(D_MODEL // MXU_COLS)]

    def gate_chunk():
        n, c = gate_chunks.pop(0)
        cols = slice(c * MXU_COLS, (c + 1) * MXU_COLS)
        gate_buf[n, :, cols] = _sigmoid(proj(_C_GATE + n * D_MODEL + c * MXU_COLS, MXU_COLS))

    qkv = proj(_C_Q, 768)
    kv_buf[pl.ds(ATT_BLOCK, tm), :] = qkv[:, 512:768]
    q = qkv[:, 0:512] * (HEAD_DIM ** -0.5)
    c3 = proj(_C_CONV, 3 * BRANCH_WIDTH)
    pp = proj(_C_POOL, BRANCH_WIDTH)
    usv = proj(_C_SGU, 2 * BRANCH_WIDTH)

    z = c3[:, 512:1024] * c3[:, 1024:1536]
    z_buf[pl.ds(CONV_HALO, tm), :] = z
    cw = conv_w_ref[...]
    conv = (cw[0:1] * z_buf[pl.ds(CONV_HALO - 2, tm), :] + cw[1:2] * z_buf[pl.ds(CONV_HALO - 1, tm), :]
            + cw[2:3] * z)
    z_buf[pl.ds(0, CONV_HALO), :] = z_buf[pl.ds(tm, CONV_HALO), :]
    y_buf[1] = (c3[:, 0:512] * conv).astype(BF16)

    p_buf[pl.ds(POOL_HALO, tm), :] = pp
    tpos = i * tm + lax.broadcasted_iota(jnp.int32, (tm, GROUP), 0)
    pooled = []
    for gi, w in enumerate(POOL_SIZES):
        cs = slice(gi * GROUP, (gi + 1) * GROUP)
        win = pp[:, cs]
        for k in range(1, w):
            win = win + p_buf[pl.ds(POOL_HALO - k, tm), cs]
        cnt = jnp.minimum(tpos + 1, w).astype(F32)
        pooled.append((win / cnt - pp[:, cs]).astype(BF16))
    p_buf[pl.ds(0, POOL_HALO), :] = p_buf[pl.ds(tm, POOL_HALO), :]

    def pool_linear():
        for gi in range(len(POOL_SIZES)):
            cs = slice(gi * GROUP, (gi + 1) * GROUP)
            y_buf[2, :, cs] = (_dot(pooled[gi], pool_w_ref[gi]) * pool_scale_ref[:, cs]).astype(BF16)

    def sgu_mix():
        tril = (lax.broadcasted_iota(jnp.int32, (GROUP, GROUP), 0)
                >= lax.broadcasted_iota(jnp.int32, (GROUP, GROUP), 1))
        for g in range(BRANCH_WIDTH // GROUP):
            wg = jnp.where(tril, sgu_w_ref[g], jnp.zeros((GROUP, GROUP), BF16))
            for c in range(nblk):
                rows = slice(c * GROUP, (c + 1) * GROUP)
                sv = usv[rows, BRANCH_WIDTH + g * GROUP:BRANCH_WIDTH + (g + 1) * GROUP].astype(BF16)
                mixed = _dot(wg, sv) + sgu_b_ref[g]
                y_buf[3, pl.ds(c * GROUP, GROUP), g * GROUP:(g + 1) * GROUP] = (
                    usv[rows, g * GROUP:(g + 1) * GROUP] * mixed).astype(BF16)

    lo128 = lax.broadcasted_iota(jnp.int32, (ATT_BLOCK, LANES), 1) < HEAD_DIM
    lo256 = lax.broadcasted_iota(jnp.int32, (2 * ATT_BLOCK, LANES), 1) < HEAD_DIM
    qi = lax.broadcasted_iota(jnp.int32, (ATT_BLOCK, 2 * ATT_BLOCK), 0)
    kj = lax.broadcasted_iota(jnp.int32, (ATT_BLOCK, 2 * ATT_BLOCK), 1)
    rel = qi + ATT_BLOCK - kj
    local = (rel >= 0) & (rel < ATT_BLOCK)

    def attention_operands(b):
        kvb = kv_buf[pl.ds(b * ATT_BLOCK, 2 * ATT_BLOCK), :]
        kb, vb = kvb[:, 0:LANES], kvb[:, LANES:2 * LANES]
        kr, vr = pltpu.roll(kb, HEAD_DIM, 1), pltpu.roll(vb, HEAD_DIM, 1)
        kkT = [jnp.where(lo256, kb, kr).T.astype(BF16), jnp.where(lo256, kr, kb).T.astype(BF16)]
        vv = [jnp.where(lo256, vb, vr).astype(BF16), jnp.where(lo256, vr, vb).astype(BF16)]
        valid = local & (kj + (i * tm + (b - 1) * ATT_BLOCK) >= 0)
        return kkT, vv, valid

    pairs = [(b, j) for b in range(nblk) for j in range(ATT_HEADS // 2)]
    block_ops = {0: attention_operands(0)}

    def pair_scores(b, j):
        kkT, _, valid = block_ops[b]
        qp = q[b * ATT_BLOCK:(b + 1) * ATT_BLOCK, j * LANES:(j + 1) * LANES]
        kvh = (2 * j) // (ATT_HEADS // ATT_KV_HEADS)
        out = []
        for half in range(2):
            keep = lo128 if half == 0 else jnp.logical_not(lo128)
            qm = jnp.where(keep, qp, 0.0).astype(BF16)
            out.append(jnp.where(valid, _dot(qm, kkT[kvh]), NEG_INF))
        return out

    scores = pair_scores(*pairs[0])
    for k, (b, j) in enumerate(pairs):
        if j == 1 and b + 1 < nblk:
            block_ops[b + 1] = attention_operands(b + 1)
        nxt = pair_scores(*pairs[k + 1]) if k + 1 < len(pairs) else None
        gate_chunk()
        vv = block_ops[b][1]
        kvh = (2 * j) // (ATT_HEADS // ATT_KV_HEADS)
        outs = []
        for half in range(2):
            s = scores[half]
            sink = sinks_ref[2 * j + half]
            m = jnp.maximum(jnp.max(s, axis=-1, keepdims=True), sink)
            p = jnp.exp(s - m)
            den = jnp.sum(p, axis=-1, keepdims=True) + jnp.exp(sink - m)
            outs.append(_dot(p.astype(BF16), vv[kvh]) / den)
        y_buf[0, pl.ds(b * ATT_BLOCK, ATT_BLOCK), j * LANES:(j + 1) * LANES] = (
            jnp.where(lo128, outs[0], outs[1]).astype(BF16))
        scores = nxt
        if (b, j) == (0, ATT_HEADS // 2 - 1):
            pool_linear()
        if (b, j) == (1, ATT_HEADS // 2 - 1):
            sgu_mix()
    kv_buf[pl.ds(0, ATT_BLOCK), :] = kv_buf[pl.ds(tm, ATT_BLOCK), :]
    while gate_chunks:
        gate_chunk()

    merged = gate_buf[0] * _dot(y_buf[0], w_br_ref[0])
    for n in range(1, N_BRANCH):
        merged = merged + gate_buf[n] * _dot(y_buf[n], w_br_ref[n])
    o_ref[...] = x + _dot(merged.astype(BF16), w_out_ref[...])


def _mixer(layer, x, ln_g, w_in, sinks, conv_w, pool_w, pool_scale, sgu_w, sgu_b_full, w_br, w_out):
    s = x.shape[0]
    tm = MIX_TM
    return pl.pallas_call(
        _mixer_body,
        grid=(s // tm,),
        in_specs=[
            pl.BlockSpec(memory_space=pltpu.SMEM),
            pl.BlockSpec((tm, D_MODEL), lambda i: (i, 0)),
            _const_spec(1, D_MODEL),
            _layer_spec(layer, D_MODEL, IN_COLS),
            _const_spec(3, BRANCH_WIDTH),
            _layer_spec(layer, 4, GROUP, GROUP),
            _const_spec(1, BRANCH_WIDTH),
            _layer_spec(layer, 4, GROUP, GROUP),
            _const_spec(4, GROUP, GROUP),
            _layer_spec(layer, N_BRANCH, BRANCH_WIDTH, D_MODEL),
            _layer_spec(layer, D_MODEL, D_MODEL),
        ],
        out_specs=pl.BlockSpec((tm, D_MODEL), lambda i: (i, 0)),
        out_shape=jax.ShapeDtypeStruct((s, D_MODEL), F32),
        scratch_shapes=[
            pltpu.VMEM((tm + ATT_BLOCK, 2 * LANES), F32),
            pltpu.VMEM((tm + CONV_HALO, BRANCH_WIDTH), F32),
            pltpu.VMEM((tm + POOL_HALO, BRANCH_WIDTH), F32),
            pltpu.VMEM((N_BRANCH, tm, BRANCH_WIDTH), BF16),
            pltpu.VMEM((N_BRANCH, tm, D_MODEL), F32),
        ],
        compiler_params=pltpu.CompilerParams(
            dimension_semantics=("arbitrary",), vmem_limit_bytes=VMEM_LIMIT_BYTES),
        name="mixer",
    )(sinks, x, ln_g, w_in, conv_w, pool_w, pool_scale, sgu_w, sgu_b_full, w_br, w_out)


def _swiglu_into(hb, wg_ref, wu_ref, wd_ref, acc_ref, base):
    for c in range(FFN_HIDDEN // FFN_FC):
        cs = slice(c * FFN_FC, (c + 1) * FFN_FC)
        g = _dot(hb, wg_ref[:, cs])
        u = _dot(hb, wu_ref[:, cs])
        a = (g * _sigmoid(g) * u).astype(BF16)
        d = _dot(a, wd_ref[cs, :])
        if c == 0:
            acc_ref[...] = d if base is None else base + d
        else:
            acc_ref[...] += d


def _dense_ffn_body(x_ref, g_ref, wg_ref, wu_ref, wd_ref, *rest):
    n_cast = (len(rest) - 1) // 2
    cast_in, o_ref, cast_out = rest[:n_cast], rest[n_cast], rest[n_cast + 1:]
    for src, dst in zip(cast_in, cast_out):
        dst[...] = src[...].astype(BF16)
    x = x_ref[...]
    hb = _rmsnorm(x, g_ref[...]).astype(BF16)
    _swiglu_into(hb, wg_ref, wu_ref, wd_ref, o_ref, x)


def _dense_ffn(layer, x, ln_g, wg, wu, wd, cast_layer, cast_weights):
    s = x.shape[0]
    tm = FFN_TM
    steps = s // tm
    cast_in_specs, cast_out_specs, cast_out_shapes = [], [], []
    for w in cast_weights:
        _, rows, cols = w.shape
        assert rows % (steps * 2 * SUBLANES) == 0
        cast_in_specs.append(pl.BlockSpec((None, rows // steps, cols), lambda i: (cast_layer, i, 0)))
        cast_out_specs.append(pl.BlockSpec((rows // steps, cols), lambda i: (i, 0)))
        cast_out_shapes.append(jax.ShapeDtypeStruct((rows, cols), BF16))
    out = pl.pallas_call(
        _dense_ffn_body,
        grid=(steps,),
        in_specs=[
            pl.BlockSpec((tm, D_MODEL), lambda i: (i, 0)),
            _const_spec(1, D_MODEL),
            _layer_spec(layer, D_MODEL, FFN_HIDDEN),
            _layer_spec(layer, D_MODEL, FFN_HIDDEN),
            _layer_spec(layer, FFN_HIDDEN, D_MODEL),
        ] + cast_in_specs,
        out_specs=[pl.BlockSpec((tm, D_MODEL), lambda i: (i, 0))] + cast_out_specs,
        out_shape=[jax.ShapeDtypeStruct((s, D_MODEL), F32)] + cast_out_shapes,
        compiler_params=pltpu.CompilerParams(
            dimension_semantics=("arbitrary",), vmem_limit_bytes=VMEM_LIMIT_BYTES),
        name="dense_ffn",
    )(x, ln_g, wg, wu, wd, *cast_weights)
    return out[0], out[1:]


def _expert_ffn_body(tile_expert_ref, n_used_ref, hs_ref, wg_ref, wu_ref, wd_ref, o_ref, acc_ref):
    del tile_expert_ref
    i = pl.program_id(0)
    tm = hs_ref.shape[0]

    @pl.when(i < n_used_ref[0])
    def _():
        hb = hs_ref[...].reshape(tm, D_MODEL).astype(BF16)
        _swiglu_into(hb, wg_ref, wu_ref, wd_ref, acc_ref, None)
        o_ref[...] = acc_ref[...].reshape(o_ref.shape)

    @pl.when(i >= n_used_ref[0])
    def _():
        o_ref[...] = jnp.zeros(o_ref.shape, F32)


def _expert_ffn(tile_expert, n_used, hs, wg, wu, wd):
    rows = hs.shape[0]
    tm = FFN_TM
    w_in_spec = pl.BlockSpec((None, D_MODEL, FFN_HIDDEN), lambda i, te, nu: (te[i], 0, 0))
    w_out_spec = pl.BlockSpec((None, FFN_HIDDEN, D_MODEL), lambda i, te, nu: (te[i], 0, 0))
    return pl.pallas_call(
        _expert_ffn_body,
        grid_spec=pltpu.PrefetchScalarGridSpec(
            num_scalar_prefetch=2,
            grid=(rows // tm,),
            in_specs=[
                pl.BlockSpec((tm,) + TOKEN_TILE, lambda i, te, nu: (jnp.minimum(i, nu[0] - 1), 0, 0)),
                w_in_spec, w_in_spec, w_out_spec,
            ],
            out_specs=pl.BlockSpec((tm,) + TOKEN_TILE, lambda i, te, nu: (i, 0, 0)),
            scratch_shapes=[pltpu.VMEM((tm, D_MODEL), F32)],
        ),
        out_shape=jax.ShapeDtypeStruct((rows,) + TOKEN_TILE, F32),
        compiler_params=pltpu.CompilerParams(
            dimension_semantics=("arbitrary",), vmem_limit_bytes=VMEM_LIMIT_BYTES),
        name="expert_ffn",
    )(tile_expert, n_used, hs, wg, wu, wd)


def _router_body(x_ref, g_ref, rw_ref, rb_ref, h_ref, meta_ref, cnt_ref, carry):
    i = pl.program_id(0)
    sub = ROUTER_SUB

    @pl.when(i == 0)
    def _():
        carry[...] = jnp.zeros(carry.shape, F32)

    w = rw_ref[...]
    w_hi = w.astype(BF16)
    w_lo = (w - w_hi.astype(F32)).astype(BF16)
    lane = lax.broadcasted_iota(jnp.int32, (sub, LANES), 1)
    below = (lax.broadcasted_iota(jnp.int32, (sub, sub), 0) > lax.broadcasted_iota(jnp.int32, (sub, sub), 1))
    below = jnp.where(below, 1.0, 0.0).astype(BF16)
    for k in range(x_ref.shape[0] // sub):
        rows = pl.ds(k * sub, sub)
        h = _rmsnorm(x_ref[rows, :], g_ref[...])
        h_ref[rows] = h.reshape((sub,) + TOKEN_TILE)
        h_hi = h.astype(BF16)
        h_lo = (h - h_hi.astype(F32)).astype(BF16)
        logits = _dot(h_hi, w_hi) + (_dot(h_lo, w_hi) + _dot(h_hi, w_lo)) + rb_ref[...]
        logits = jnp.where(lane < N_EXPERTS, logits, NEG_INF)
        m1 = jnp.max(logits, axis=-1, keepdims=True)
        i1 = jnp.min(jnp.where(logits == m1, lane, LANES), axis=-1, keepdims=True)
        rest = jnp.where(lane == i1, NEG_INF, logits)
        m2 = jnp.max(rest, axis=-1, keepdims=True)
        i2 = jnp.min(jnp.where(rest == m2, lane, LANES), axis=-1, keepdims=True)
        e = jnp.exp(m2 - m1)
        w1 = 1.0 / (1.0 + e)
        w2 = e / (1.0 + e)
        oh1, oh2 = lane == i1, lane == i2
        cnt = jnp.where(oh1 | oh2, 1.0, 0.0)
        prefix = _dot(below, cnt.astype(BF16)) + carry[0:1, :]
        r1 = jnp.sum(jnp.where(oh1, prefix, 0.0), axis=-1, keepdims=True)
        r2 = jnp.sum(jnp.where(oh2, prefix, 0.0), axis=-1, keepdims=True)
        meta = jnp.zeros((sub, LANES), F32)
        for n, v in enumerate((i1.astype(F32), i2.astype(F32), w1, w2, r1, r2)):
            meta = jnp.where(lane == n, v, meta)
        meta_ref[rows, :] = meta
        carry[...] += jnp.sum(cnt, axis=0, keepdims=True)
    cnt_ref[...] = carry[...]


def _router(x, ln_g, rw_pad, rb_pad):
    s = x.shape[0]
    tm = ROUTER_TM
    return pl.pallas_call(
        _router_body,
        grid=(s // tm,),
        in_specs=[
            pl.BlockSpec((tm, D_MODEL), lambda i: (i, 0)),
            _const_spec(1, D_MODEL),
            _const_spec(D_MODEL, LANES),
            _const_spec(1, LANES),
        ],
        out_specs=[
            pl.BlockSpec((tm,) + TOKEN_TILE, lambda i: (i, 0, 0)),
            pl.BlockSpec((tm, LANES), lambda i: (i, 0)),
            pl.BlockSpec((SUBLANES, LANES), lambda i: (0, 0)),
        ],
        out_shape=[
            jax.ShapeDtypeStruct((s,) + TOKEN_TILE, F32),
            jax.ShapeDtypeStruct((s, LANES), F32),
            jax.ShapeDtypeStruct((SUBLANES, LANES), F32),
        ],
        scratch_shapes=[pltpu.VMEM((SUBLANES, LANES), F32)],
        compiler_params=pltpu.CompilerParams(
            dimension_semantics=("arbitrary",), vmem_limit_bytes=VMEM_LIMIT_BYTES),
        name="router",
    )(x, ln_g, rw_pad, rb_pad)


def _token_copy(src_ref, src_row, dst_ref, dst_row, sem):
    return pltpu.make_async_copy(src_ref.at[src_row], dst_ref.at[dst_row], sem)


def _gather_body(dest_ref, fill_start_ref, fill_on_ref, h_ref, hs_ref, zero_buf, sem):
    i = pl.program_id(0)

    def fill_copy(k):
        start = pl.multiple_of(fill_start_ref[k], FFN_TM)
        return pltpu.make_async_copy(zero_buf, hs_ref.at[pl.ds(start, FFN_TM)], sem)

    @pl.when(i == 0)
    def _():
        zero_buf[...] = jnp.zeros(zero_buf.shape, F32)
        for k in range(2 * N_EXPERTS):
            @pl.when(fill_on_ref[k] > 0)
            def _():
                fill_copy(k).start()
        for k in range(2 * N_EXPERTS):
            @pl.when(fill_on_ref[k] > 0)
            def _():
                fill_copy(k).wait()

    def start(r, c):
        t = i * GATHER_T + r
        _token_copy(h_ref, r, hs_ref, dest_ref[2 * t], sem).start(priority=0)
        _token_copy(h_ref, r, hs_ref, dest_ref[2 * t + 1], sem).start(priority=1)
        return c

    lax.fori_loop(0, GATHER_T, start, 0, unroll=8)

    def wait(r, c):
        _token_copy(h_ref, 0, hs_ref, 0, sem).wait()
        return c

    lax.fori_loop(0, 2 * GATHER_T, wait, 0, unroll=8)


def _gather_rows(dest, fill_start, fill_on, h, rows):
    s = h.shape[0]
    return pl.pallas_call(
        _gather_body,
        grid_spec=pltpu.PrefetchScalarGridSpec(
            num_scalar_prefetch=3,
            grid=(s // GATHER_T,),
            in_specs=[pl.BlockSpec((GATHER_T,) + TOKEN_TILE, lambda i, *_: (i, 0, 0))],
            out_specs=pl.BlockSpec(memory_space=pl.ANY),
            scratch_shapes=[pltpu.VMEM((FFN_TM,) + TOKEN_TILE, F32), pltpu.SemaphoreType.DMA],
        ),
        out_shape=jax.ShapeDtypeStruct((rows,) + TOKEN_TILE, F32),
        compiler_params=pltpu.CompilerParams(dimension_semantics=("arbitrary",)),
        name="gather_rows",
    )(dest, fill_start, fill_on, h)


def _combine_body(dest_ref, x_ref, meta_ref, ys_ref, *rest, final_norm):
    if final_norm:
        g_ref, o_ref, y_buf, sems = rest
    else:
        o_ref, y_buf, sems = rest
    i = pl.program_id(0)
    slot = i % 2

    def issue(step, slot_):
        def start(r, c):
            t = step * COMBINE_T + r
            _token_copy(ys_ref, dest_ref[2 * t], y_buf.at[slot_, 0], r, sems.at[slot_]).start(priority=0)
            _token_copy(ys_ref, dest_ref[2 * t + 1], y_buf.at[slot_, 1], r, sems.at[slot_]).start(priority=1)
            return c

        lax.fori_loop(0, COMBINE_T, start, 0, unroll=8)

    @pl.when(i == 0)
    def _():
        issue(0, 0)

    @pl.when(i + 1 < pl.num_programs(0))
    def _():
        issue(i + 1, 1 - slot)

    def wait(r, c):
        _token_copy(ys_ref, 0, y_buf.at[slot, 0], 0, sems.at[slot]).wait()
        return c

    lax.fori_loop(0, 2 * COMBINE_T, wait, 0, unroll=8)

    meta = meta_ref[...]
    lane = lax.broadcasted_iota(jnp.int32, meta.shape, 1)
    w1 = jnp.sum(jnp.where(lane == 2, meta, 0.0), axis=-1, keepdims=True)
    w2 = jnp.sum(jnp.where(lane == 3, meta, 0.0), axis=-1, keepdims=True)
    y1 = y_buf[slot, 0].reshape(COMBINE_T, D_MODEL)
    y2 = y_buf[slot, 1].reshape(COMBINE_T, D_MODEL)
    y = x_ref[...] + (w1 * y1 + w2 * y2)
    o_ref[...] = _rmsnorm(y, g_ref[...]) if final_norm else y


def _combine(dest, x, meta, ys, final_g=None):
    s = x.shape[0]
    t = COMBINE_T
    in_specs = [
        pl.BlockSpec((t, D_MODEL), lambda i, d: (i, 0)),
        pl.BlockSpec((t, LANES), lambda i, d: (i, 0)),
        pl.BlockSpec(memory_space=pl.ANY),
    ]
    args = [dest, x, meta, ys]
    if final_g is not None:
        in_specs.append(pl.BlockSpec((1, D_MODEL), lambda i, d: (0, 0)))
        args.append(final_g)
    return pl.pallas_call(
        functools.partial(_combine_body, final_norm=final_g is not None),
        grid_spec=pltpu.PrefetchScalarGridSpec(
            num_scalar_prefetch=1,
            grid=(s // t,),
            in_specs=in_specs,
            out_specs=pl.BlockSpec((t, D_MODEL), lambda i, d: (i, 0)),
            scratch_shapes=[pltpu.VMEM((2, 2, t) + TOKEN_TILE, F32), pltpu.SemaphoreType.DMA((2,))],
        ),
        out_shape=jax.ShapeDtypeStruct((s, D_MODEL), F32),
        compiler_params=pltpu.CompilerParams(
            dimension_semantics=("arbitrary",), vmem_limit_bytes=VMEM_LIMIT_BYTES),
        name="combine",
    )(*args)


def _moe_ffn(x, ln_g, router_w, router_b, wg, wu, wd, final_g):
    s = x.shape[0]
    tm = FFN_TM
    rw_pad = jnp.zeros((D_MODEL, LANES), F32).at[:, :N_EXPERTS].set(router_w)
    rb_pad = jnp.zeros((1, LANES), F32).at[0, :N_EXPERTS].set(router_b)
    h, meta, counts = _router(x, ln_g, rw_pad, rb_pad)
    idx = meta[:, 0:2].astype(jnp.int32)
    rank = meta[:, 4:6].astype(jnp.int32)
    cnt = counts[0, :N_EXPERTS].astype(jnp.int32)
    padded = (cnt + tm - 1) // tm * tm
    ends = jnp.cumsum(padded)
    dest = ((ends - padded)[idx] + rank).reshape(-1)
    n_tiles = 2 * s // tm + N_EXPERTS
    tile_start = jnp.arange(n_tiles, dtype=jnp.int32) * tm
    tile_expert = jnp.minimum(
        jnp.sum(tile_start[:, None] >= ends[None, :], axis=1), N_EXPERTS - 1).astype(jnp.int32)
    n_used = (ends[-1:] // tm).astype(jnp.int32)
    spare = ends[-1] + jnp.arange(N_EXPERTS, dtype=jnp.int32) * tm
    fill_start = jnp.concatenate([ends - tm, spare]).astype(jnp.int32)
    fill_on = jnp.concatenate([padded > 0, spare < n_tiles * tm]).astype(jnp.int32)
    fill_start = jnp.where(fill_on > 0, fill_start, 0)
    hs = _gather_rows(dest, fill_start, fill_on, h, n_tiles * tm)
    ys = _expert_ffn(tile_expert, n_used, hs, wg, wu, wd)
    return _combine(dest, x, meta, ys, final_g)


def kernel(x, ln_mix_g, w_in, attn_sinks, conv_w, pool_w, pool_scale, sgu_w, sgu_b, w_branch, w_out,
           ln_ffn_g, dense_w_gate, dense_w_up, dense_w_down, router_w, router_b,
           moe_w_gate, moe_w_up, moe_w_down, ln_final_g):
    batch, seq, _ = x.shape
    depth = w_in.shape[0]
    assert batch == 1 and seq % MIX_TM == 0 and seq % FFN_TM == 0 and depth % 2 == 0
    assert D_MODEL == SUBLANES * LANES
    xs = x.reshape(seq, D_MODEL)
    w_in, pool_w, sgu_w, w_branch, w_out = (w.astype(BF16) for w in (w_in, pool_w, sgu_w, w_branch, w_out))
    dense_w = [w.astype(BF16) for w in (dense_w_gate, dense_w_up, dense_w_down)]
    n_moe = moe_w_gate.shape[0]
    moe_w_f32 = [w.reshape(n_moe, -1, w.shape[-1]) for w in (moe_w_gate, moe_w_up, moe_w_down)]
    for layer in range(depth):
        sgu_b_full = jnp.broadcast_to(sgu_b[layer][:, :, None], (4, GROUP, GROUP))
        xs = _mixer(layer, xs, ln_mix_g[layer][None], w_in, attn_sinks[layer], conv_w[layer], pool_w,
                    pool_scale[layer][None], sgu_w, sgu_b_full, w_branch, w_out)
        i = layer // 2
        if layer % 2 == 0:
            xs, moe_w = _dense_ffn(i, xs, ln_ffn_g[layer][None], *dense_w, i, moe_w_f32)
            moe_w = [w.reshape(N_EXPERTS, -1, w.shape[-1]) for w in moe_w]
        else:
            final_g = ln_final_g[None] if layer == depth - 1 else None
            xs = _moe_ffn(xs, ln_ffn_g[layer][None], router_w[i], router_b[i], *moe_w, final_g)
    return xs.reshape(batch, seq, D_MODEL)
```

```python
import functools

import jax
import jax.numpy as jnp
from jax import lax
from jax.experimental import pallas as pl
from jax.experimental.pallas import tpu as pltpu

F32 = jnp.float32
BF16 = jnp.bfloat16

D_MODEL = 1024
ATT_HEADS = 8
ATT_KV_HEADS = 2
HEAD_DIM = 64
ATT_BLOCK = 128
POOL_SIZES = (2, 4, 8, 16)
GROUP = 128
N_BRANCH = 4
BRANCH_WIDTH = 512
FFN_HIDDEN = 2816
N_EXPERTS = 8
RMS_EPS = 1e-6
NEG_INF = -1e30

_C_Q, _C_KV, _C_CONV, _C_POOL, _C_SGU, _C_GATE = 0, 512, 768, 2304, 2816, 3840
IN_COLS = _C_GATE + N_BRANCH * D_MODEL

LANES = 128
SUBLANES = 8
TOKEN_TILE = (SUBLANES, LANES)
MXU_COLS = 256
VMEM_LIMIT_BYTES = 60 * 1024 * 1024

MIX_TM = 512
FFN_TM = 512
FFN_FC = MXU_COLS
ROUTER_TM = 1024
ROUTER_SUB = 512
GATHER_T = 512
COMBINE_T = 256
CONV_HALO = SUBLANES
POOL_HALO = 2 * SUBLANES


def _rmsnorm(x, g):
    return x * lax.rsqrt(jnp.mean(x * x, axis=-1, keepdims=True) + RMS_EPS) * g


def _sigmoid(x):
    return 1.0 / (1.0 + jnp.exp(-x))


def _dot(a, b):
    return jnp.dot(a, b, preferred_element_type=F32)


def _const_spec(*shape):
    return pl.BlockSpec(shape, lambda *_: (0,) * len(shape), pipeline_mode=pl.Buffered(1))


def _layer_spec(layer, *shape):
    return pl.BlockSpec((None,) + shape, lambda *_: (layer,) + (0,) * len(shape),
                        pipeline_mode=pl.Buffered(1))


def _mixer_body(sinks_ref, x_ref, g_ref, w_in_ref, conv_w_ref, pool_w_ref, pool_scale_ref,
                sgu_w_ref, sgu_b_ref, w_br_ref, w_out_ref, o_ref,
                kv_buf, z_buf, p_buf, y_buf, gate_buf):
    tm = x_ref.shape[0]
    nblk = tm // ATT_BLOCK
    i = pl.program_id(0)

    @pl.when(i == 0)
    def _():
        kv_buf[pl.ds(0, ATT_BLOCK), :] = jnp.zeros((ATT_BLOCK, kv_buf.shape[1]), F32)
        z_buf[pl.ds(0, CONV_HALO), :] = jnp.zeros((CONV_HALO, z_buf.shape[1]), F32)
        p_buf[pl.ds(0, POOL_HALO), :] = jnp.zeros((POOL_HALO, p_buf.shape[1]), F32)

    x = x_ref[...]
    hb = _rmsnorm(x, g_ref[...]).astype(BF16)

    def proj(c0, n):
        return _dot(hb, w_in_ref[:, c0:c0 + n])

    gate_chunks = [(n, c) for n in range(N_BRANCH) for c in range(D_MODEL // MXU_COLS)]

    def gate_chunk():
        n, c = gate_chunks.pop(0)
        cols = slice(c * MXU_COLS, (c + 1) * MXU_COLS)
        gate_buf[n, :, cols] = _sigmoid(proj(_C_GATE + n * D_MODEL + c * MXU_COLS, MXU_COLS))

    qkv = proj(_C_Q, 768)
    kv_buf[pl.ds(ATT_BLOCK, tm), :] = qkv[:, 512:768]
    q = qkv[:, 0:512] * (HEAD_DIM ** -0.5)
    c3 = proj(_C_CONV, 3 * BRANCH_WIDTH)
    pp = proj(_C_POOL, BRANCH_WIDTH)
    usv = proj(_C_SGU, 2 * BRANCH_WIDTH)

    z = c3[:, 512:1024] * c3[:, 1024:1536]
    z_buf[pl.ds(CONV_HALO, tm), :] = z
    cw = conv_w_ref[...]
    conv = (cw[0:1] * z_buf[pl.ds(CONV_HALO - 2, tm), :] + cw[1:2] * z_buf[pl.ds(CONV_HALO - 1, tm), :]
            + cw[2:3] * z)
    z_buf[pl.ds(0, CONV_HALO), :] = z_buf[pl.ds(tm, CONV_HALO), :]
    y_buf[1] = (c3[:, 0:512] * conv).astype(BF16)

    p_buf[pl.ds(POOL_HALO, tm), :] = pp
    tpos = i * tm + lax.broadcasted_iota(jnp.int32, (tm, GROUP), 0)
    pooled = []
    for gi, w in enumerate(POOL_SIZES):
        cs = slice(gi * GROUP, (gi + 1) * GROUP)
        win = pp[:, cs]
        for k in range(1, w):
            win = win + p_buf[pl.ds(POOL_HALO - k, tm), cs]
        cnt = jnp.minimum(tpos + 1, w).astype(F32)
        pooled.append((win / cnt - pp[:, cs]).astype(BF16))
    p_buf[pl.ds(0, POOL_HALO), :] = p_buf[pl.ds(tm, POOL_HALO), :]

    def pool_linear():
        for gi in range(len(POOL_SIZES)):
            cs = slice(gi * GROUP, (gi + 1) * GROUP)
            y_buf[2, :, cs] = (_dot(pooled[gi], pool_w_ref[gi]) * pool_scale_ref[:, cs]).astype(BF16)

    def sgu_mix():
        tril = (lax.broadcasted_iota(jnp.int32, (GROUP, GROUP), 0)
                >= lax.broadcasted_iota(jnp.int32, (GROUP, GROUP), 1))
        for g in range(BRANCH_WIDTH // GROUP):
            wg = jnp.where(tril, sgu_w_ref[g], jnp.zeros((GROUP, GROUP), BF16))
            for c in range(nblk):
                rows = slice(c * GROUP, (c + 1) * GROUP)
                sv = usv[rows, BRANCH_WIDTH + g * GROUP:BRANCH_WIDTH + (g + 1) * GROUP].astype(BF16)
                mixed = _dot(wg, sv) + sgu_b_ref[g]
                y_buf[3, pl.ds(c * GROUP, GROUP), g * GROUP:(g + 1) * GROUP] = (
                    usv[rows, g * GROUP:(g + 1) * GROUP] * mixed).astype(BF16)

    lo128 = lax.broadcasted_iota(jnp.int32, (ATT_BLOCK, LANES), 1) < HEAD_DIM
    lo256 = lax.broadcasted_iota(jnp.int32, (2 * ATT_BLOCK, LANES), 1) < HEAD_DIM
    qi = lax.broadcasted_iota(jnp.int32, (ATT_BLOCK, 2 * ATT_BLOCK), 0)
    kj = lax.broadcasted_iota(jnp.int32, (ATT_BLOCK, 2 * ATT_BLOCK), 1)
    rel = qi + ATT_BLOCK - kj
    local = (rel >= 0) & (rel < ATT_BLOCK)

    def attention_operands(b):
        kvb = kv_buf[pl.ds(b * ATT_BLOCK, 2 * ATT_BLOCK), :]
        kb, vb = kvb[:, 0:LANES], kvb[:, LANES:2 * LANES]
        kr, vr = pltpu.roll(kb, HEAD_DIM, 1), pltpu.roll(vb, HEAD_DIM, 1)
        kkT = [jnp.where(lo256, kb, kr).T.astype(BF16), jnp.where(lo256, kr, kb).T.astype(BF16)]
        vv = [jnp.where(lo256, vb, vr).astype(BF16), jnp.where(lo256, vr, vb).astype(BF16)]
        valid = local & (kj + (i * tm + (b - 1) * ATT_BLOCK) >= 0)
        return kkT, vv, valid

    pairs = [(b, j) for b in range(nblk) for j in range(ATT_HEADS // 2)]
    block_ops = {0: attention_operands(0)}

    def pair_scores(b, j):
        kkT, _, valid = block_ops[b]
        qp = q[b * ATT_BLOCK:(b + 1) * ATT_BLOCK, j * LANES:(j + 1) * LANES]
        kvh = (2 * j) // (ATT_HEADS // ATT_KV_HEADS)
        out = []
        for half in range(2):
            keep = lo128 if half == 0 else jnp.logical_not(lo128)
            qm = jnp.where(keep, qp, 0.0).astype(BF16)
            out.append(jnp.where(valid, _dot(qm, kkT[kvh]), NEG_INF))
        return out

    scores = pair_scores(*pairs[0])
    for k, (b, j) in enumerate(pairs):
        if j == 1 and b + 1 < nblk:
            block_ops[b + 1] = attention_operands(b + 1)
        nxt = pair_scores(*pairs[k + 1]) if k + 1 < len(pairs) else None
        gate_chunk()
        vv = block_ops[b][1]
        kvh = (2 * j) // (ATT_HEADS // ATT_KV_HEADS)
        outs = []
        for half in range(2):
            s = scores[half]
            sink = sinks_ref[2 * j + half]
            m = jnp.maximum(jnp.max(s, axis=-1, keepdims=True), sink)
            p = jnp.exp(s - m)
            den = jnp.sum(p, axis=-1, keepdims=True) + jnp.exp(sink - m)
            outs.append(_dot(p.astype(BF16), vv[kvh]) / den)
        y_buf[0, pl.ds(b * ATT_BLOCK, ATT_BLOCK), j * LANES:(j + 1) * LANES] = (
            jnp.where(lo128, outs[0], outs[1]).astype(BF16))
        scores = nxt
        if (b, j) == (0, ATT_HEADS // 2 - 1):
            pool_linear()
        if (b, j) == (1, ATT_HEADS // 2 - 1):
            sgu_mix()
    kv_buf[pl.ds(0, ATT_BLOCK), :] = kv_buf[pl.ds(tm, ATT_BLOCK), :]
    while gate_chunks:
        gate_chunk()

    merged = gate_buf[0] * _dot(y_buf[0], w_br_ref[0])
    for n in range(1, N_BRANCH):
        merged = merged + gate_buf[n] * _dot(y_buf[n], w_br_ref[n])
    o_ref[...] = x + _dot(merged.astype(BF16), w_out_ref[...])


def _mixer(layer, x, ln_g, w_in, sinks, conv_w, pool_w, pool_scale, sgu_w, sgu_b_full, w_br, w_out):
    s = x.shape[0]
    tm = MIX_TM
    return pl.pallas_call(
        _mixer_body,
        grid=(s // tm,),
        in_specs=[
            pl.BlockSpec(memory_space=pltpu.SMEM),
            pl.BlockSpec((tm, D_MODEL), lambda i: (i, 0)),
            _const_spec(1, D_MODEL),
            _layer_spec(layer, D_MODEL, IN_COLS),
            _const_spec(3, BRANCH_WIDTH),
            _layer_spec(layer, 4, GROUP, GROUP),
            _const_spec(1, BRANCH_WIDTH),
            _layer_spec(layer, 4, GROUP, GROUP),
            _const_spec(4, GROUP, GROUP),
            _layer_spec(layer, N_BRANCH, BRANCH_WIDTH, D_MODEL),
            _layer_spec(layer, D_MODEL, D_MODEL),
        ],
        out_specs=pl.BlockSpec((tm, D_MODEL), lambda i: (i, 0)),
        out_shape=jax.ShapeDtypeStruct((s, D_MODEL), F32),
        scratch_shapes=[
            pltpu.VMEM((tm + ATT_BLOCK, 2 * LANES), F32),
            pltpu.VMEM((tm + CONV_HALO, BRANCH_WIDTH), F32),
            pltpu.VMEM((tm + POOL_HALO, BRANCH_WIDTH), F32),
            pltpu.VMEM((N_BRANCH, tm, BRANCH_WIDTH), BF16),
            pltpu.VMEM((N_BRANCH, tm, D_MODEL), F32),
        ],
        compiler_params=pltpu.CompilerParams(
            dimension_semantics=("arbitrary",), vmem_limit_bytes=VMEM_LIMIT_BYTES),
        name="mixer",
    )(sinks, x, ln_g, w_in, conv_w, pool_w, pool_scale, sgu_w, sgu_b_full, w_br, w_out)


def _swiglu_into(hb, wg_ref, wu_ref, wd_ref, acc_ref, base):
    for c in range(FFN_HIDDEN // FFN_FC):
        cs = slice(c * FFN_FC, (c + 1) * FFN_FC)
        g = _dot(hb, wg_ref[:, cs])
        u = _dot(hb, wu_ref[:, cs])
        a = (g * _sigmoid(g) * u).astype(BF16)
        d = _dot(a, wd_ref[cs, :])
        if c == 0:
            acc_ref[...] = d if base is None else base + d
        else:
            acc_ref[...] += d


def _dense_ffn_body(x_ref, g_ref, wg_ref, wu_ref, wd_ref, *rest):
    n_cast = (len(rest) - 1) // 2
    cast_in, o_ref, cast_out = rest[:n_cast], rest[n_cast], rest[n_cast + 1:]
    for src, dst in zip(cast_in, cast_out):
        dst[...] = src[...].astype(BF16)
    x = x_ref[...]
    hb = _rmsnorm(x, g_ref[...]).astype(BF16)
    _swiglu_into(hb, wg_ref, wu_ref, wd_ref, o_ref, x)


def _dense_ffn(layer, x, ln_g, wg, wu, wd, cast_layer, cast_weights):
    s = x.shape[0]
    tm = FFN_TM
    steps = s // tm
    cast_in_specs, cast_out_specs, cast_out_shapes = [], [], []
    for w in cast_weights:
        _, rows, cols = w.shape
        assert rows % (steps * 2 * SUBLANES) == 0
        cast_in_specs.append(pl.BlockSpec((None, rows // steps, cols), lambda i: (cast_layer, i, 0)))
        cast_out_specs.append(pl.BlockSpec((rows // steps, cols), lambda i: (i, 0)))
        cast_out_shapes.append(jax.ShapeDtypeStruct((rows, cols), BF16))
    out = pl.pallas_call(
        _dense_ffn_body,
        grid=(steps,),
        in_specs=[
            pl.BlockSpec((tm, D_MODEL), lambda i: (i, 0)),
            _const_spec(1, D_MODEL),
            _layer_spec(layer, D_MODEL, FFN_HIDDEN),
            _layer_spec(layer, D_MODEL, FFN_HIDDEN),
            _layer_spec(layer, FFN_HIDDEN, D_MODEL),
        ] + cast_in_specs,
        out_specs=[pl.BlockSpec((tm, D_MODEL), lambda i: (i, 0))] + cast_out_specs,
        out_shape=[jax.ShapeDtypeStruct((s, D_MODEL), F32)] + cast_out_shapes,
        compiler_params=pltpu.CompilerParams(
            dimension_semantics=("arbitrary",), vmem_limit_bytes=VMEM_LIMIT_BYTES),
        name="dense_ffn",
    )(x, ln_g, wg, wu, wd, *cast_weights)
    return out[0], out[1:]


def _expert_ffn_body(tile_expert_ref, n_used_ref, tile_valid_ref, hs_ref, wg_ref, wu_ref, wd_ref,
                     o_ref, acc_ref):
    del tile_expert_ref
    i = pl.program_id(0)
    tm = hs_ref.shape[0]
    half = tm // 2
    used = i < n_used_ref[0]
    small = tile_valid_ref[i] <= half

    def run(rows):
        hb = hs_ref[pl.ds(0, rows)].reshape(rows, D_MODEL).astype(BF16)
        acc = acc_ref.at[pl.ds(0, rows)]
        _swiglu_into(hb, wg_ref, wu_ref, wd_ref, acc, None)
        o_ref[pl.ds(0, rows)] = acc[...].reshape((rows,) + TOKEN_TILE)

    @pl.when(used & jnp.logical_not(small))
    def _():
        run(tm)

    @pl.when(used & small)
    def _():
        run(half)
        o_ref[pl.ds(half, half)] = jnp.zeros((half,) + TOKEN_TILE, F32)

    @pl.when(jnp.logical_not(used))
    def _():
        o_ref[...] = jnp.zeros(o_ref.shape, F32)


def _expert_ffn(tile_expert, n_used, tile_valid, hs, wg, wu, wd):
    rows = hs.shape[0]
    tm = FFN_TM
    w_in_spec = pl.BlockSpec((None, D_MODEL, FFN_HIDDEN), lambda i, te, nu, tv: (te[i], 0, 0))
    w_out_spec = pl.BlockSpec((None, FFN_HIDDEN, D_MODEL), lambda i, te, nu, tv: (te[i], 0, 0))
    return pl.pallas_call(
        _expert_ffn_body,
        grid_spec=pltpu.PrefetchScalarGridSpec(
            num_scalar_prefetch=3,
            grid=(rows // tm,),
            in_specs=[
                pl.BlockSpec((tm,) + TOKEN_TILE, lambda i, te, nu, tv: (jnp.minimum(i, nu[0] - 1), 0, 0)),
                w_in_spec, w_in_spec, w_out_spec,
            ],
            out_specs=pl.BlockSpec((tm,) + TOKEN_TILE, lambda i, te, nu, tv: (i, 0, 0)),
            scratch_shapes=[pltpu.VMEM((tm, D_MODEL), F32)],
        ),
        out_shape=jax.ShapeDtypeStruct((rows,) + TOKEN_TILE, F32),
        compiler_params=pltpu.CompilerParams(
            dimension_semantics=("arbitrary",), vmem_limit_bytes=VMEM_LIMIT_BYTES),
        name="expert_ffn",
    )(tile_expert, n_used, tile_valid, hs, wg, wu, wd)


def _router_body(x_ref, g_ref, rw_ref, rb_ref, h_ref, meta_ref, cnt_ref, carry):
    i = pl.program_id(0)
    sub = ROUTER_SUB

    @pl.when(i == 0)
    def _():
        carry[...] = jnp.zeros(carry.shape, F32)

    w = rw_ref[...]
    w_hi = w.astype(BF16)
    w_lo = (w - w_hi.astype(F32)).astype(BF16)
    lane = lax.broadcasted_iota(jnp.int32, (sub, LANES), 1)
    below = (lax.broadcasted_iota(jnp.int32, (sub, sub), 0) > lax.broadcasted_iota(jnp.int32, (sub, sub), 1))
    below = jnp.where(below, 1.0, 0.0).astype(BF16)
    for k in range(x_ref.shape[0] // sub):
        rows = pl.ds(k * sub, sub)
        h = _rmsnorm(x_ref[rows, :], g_ref[...])
        h_ref[rows] = h.reshape((sub,) + TOKEN_TILE)
        h_hi = h.astype(BF16)
        h_lo = (h - h_hi.astype(F32)).astype(BF16)
        logits = _dot(h_hi, w_hi) + (_dot(h_lo, w_hi) + _dot(h_hi, w_lo)) + rb_ref[...]
        logits = jnp.where(lane < N_EXPERTS, logits, NEG_INF)
        m1 = jnp.max(logits, axis=-1, keepdims=True)
        i1 = jnp.min(jnp.where(logits == m1, lane, LANES), axis=-1, keepdims=True)
        rest = jnp.where(lane == i1, NEG_INF, logits)
        m2 = jnp.max(rest, axis=-1, keepdims=True)
        i2 = jnp.min(jnp.where(rest == m2, lane, LANES), axis=-1, keepdims=True)
        e = jnp.exp(m2 - m1)
        w1 = 1.0 / (1.0 + e)
        w2 = e / (1.0 + e)
        oh1, oh2 = lane == i1, lane == i2
        cnt = jnp.where(oh1 | oh2, 1.0, 0.0)
        prefix = _dot(below, cnt.astype(BF16)) + carry[0:1, :]
        r1 = jnp.sum(jnp.where(oh1, prefix, 0.0), axis=-1, keepdims=True)
        r2 = jnp.sum(jnp.where(oh2, prefix, 0.0), axis=-1, keepdims=True)
        meta = jnp.zeros((sub, LANES), F32)
        for n, v in enumerate((i1.astype(F32), i2.astype(F32), w1, w2, r1, r2)):
            meta = jnp.where(lane == n, v, meta)
        meta_ref[rows, :] = meta
        carry[...] += jnp.sum(cnt, axis=0, keepdims=True)
    cnt_ref[...] = carry[...]


def _router(x, ln_g, rw_pad, rb_pad):
    s = x.shape[0]
    tm = ROUTER_TM
    return pl.pallas_call(
        _router_body,
        grid=(s // tm,),
        in_specs=[
            pl.BlockSpec((tm, D_MODEL), lambda i: (i, 0)),
            _const_spec(1, D_MODEL),
            _const_spec(D_MODEL, LANES),
            _const_spec(1, LANES),
        ],
        out_specs=[
            pl.BlockSpec((tm,) + TOKEN_TILE, lambda i: (i, 0, 0)),
            pl.BlockSpec((tm, LANES), lambda i: (i, 0)),
            pl.BlockSpec((SUBLANES, LANES), lambda i: (0, 0)),
        ],
        out_shape=[
            jax.ShapeDtypeStruct((s,) + TOKEN_TILE, F32),
            jax.ShapeDtypeStruct((s, LANES), F32),
            jax.ShapeDtypeStruct((SUBLANES, LANES), F32),
        ],
        scratch_shapes=[pltpu.VMEM((SUBLANES, LANES), F32)],
        compiler_params=pltpu.CompilerParams(
            dimension_semantics=("arbitrary",), vmem_limit_bytes=VMEM_LIMIT_BYTES),
        name="router",
    )(x, ln_g, rw_pad, rb_pad)


def _token_copy(src_ref, src_row, dst_ref, dst_row, sem):
    return pltpu.make_async_copy(src_ref.at[src_row], dst_ref.at[dst_row], sem)


def _gather_body(dest_ref, fill_start_ref, fill_on_ref, h_ref, hs_ref, zero_buf, sem):
    i = pl.program_id(0)

    def fill_copy(k):
        start = pl.multiple_of(fill_start_ref[k], FFN_TM)
        return pltpu.make_async_copy(zero_buf, hs_ref.at[pl.ds(start, FFN_TM)], sem)

    @pl.when(i == 0)
    def _():
        zero_buf[...] = jnp.zeros(zero_buf.shape, F32)
        for k in range(2 * N_EXPERTS):
            @pl.when(fill_on_ref[k] > 0)
            def _():
                fill_copy(k).start()
        for k in range(2 * N_EXPERTS):
            @pl.when(fill_on_ref[k] > 0)
            def _():
                fill_copy(k).wait()

    def start(r, c):
        t = i * GATHER_T + r
        _token_copy(h_ref, r, hs_ref, dest_ref[2 * t], sem).start(priority=0)
        _token_copy(h_ref, r, hs_ref, dest_ref[2 * t + 1], sem).start(priority=1)
        return c

    lax.fori_loop(0, GATHER_T, start, 0, unroll=8)

    def wait(r, c):
        _token_copy(h_ref, 0, hs_ref, 0, sem).wait()
        return c

    lax.fori_loop(0, 2 * GATHER_T, wait, 0, unroll=8)


def _gather_rows(dest, fill_start, fill_on, h, rows):
    s = h.shape[0]
    return pl.pallas_call(
        _gather_body,
        grid_spec=pltpu.PrefetchScalarGridSpec(
            num_scalar_prefetch=3,
            grid=(s // GATHER_T,),
            in_specs=[pl.BlockSpec((GATHER_T,) + TOKEN_TILE, lambda i, *_: (i, 0, 0))],
            out_specs=pl.BlockSpec(memory_space=pl.ANY),
            scratch_shapes=[pltpu.VMEM((FFN_TM,) + TOKEN_TILE, F32), pltpu.SemaphoreType.DMA],
        ),
        out_shape=jax.ShapeDtypeStruct((rows,) + TOKEN_TILE, F32),
        compiler_params=pltpu.CompilerParams(dimension_semantics=("arbitrary",)),
        name="gather_rows",
    )(dest, fill_start, fill_on, h)


def _combine_body(dest_ref, x_ref, meta_ref, ys_ref, *rest, final_norm):
    if final_norm:
        g_ref, o_ref, y_buf, sems = rest
    else:
        o_ref, y_buf, sems = rest
    i = pl.program_id(0)
    slot = i % 2

    def issue(step, slot_):
        def start(r, c):
            t = step * COMBINE_T + r
            _token_copy(ys_ref, dest_ref[2 * t], y_buf.at[slot_, 0], r, sems.at[slot_]).start(priority=0)
            _token_copy(ys_ref, dest_ref[2 * t + 1], y_buf.at[slot_, 1], r, sems.at[slot_]).start(priority=1)
            return c

        lax.fori_loop(0, COMBINE_T, start, 0, unroll=8)

    @pl.when(i == 0)
    def _():
        issue(0, 0)

    @pl.when(i + 1 < pl.num_programs(0))
    def _():
        issue(i + 1, 1 - slot)

    def wait(r, c):
        _token_copy(ys_ref, 0, y_buf.at[slot, 0], 0, sems.at[slot]).wait()
        return c

    lax.fori_loop(0, 2 * COMBINE_T, wait, 0, unroll=8)

    meta = meta_ref[...]
    lane = lax.broadcasted_iota(jnp.int32, meta.shape, 1)
    w1 = jnp.sum(jnp.where(lane == 2, meta, 0.0), axis=-1, keepdims=True)
    w2 = jnp.sum(jnp.where(lane == 3, meta, 0.0), axis=-1, keepdims=True)
    y1 = y_buf[slot, 0].reshape(COMBINE_T, D_MODEL)
    y2 = y_buf[slot, 1].reshape(COMBINE_T, D_MODEL)
    y = x_ref[...] + (w1 * y1 + w2 * y2)
    o_ref[...] = _rmsnorm(y, g_ref[...]) if final_norm else y


def _combine(dest, x, meta, ys, final_g=None):
    s = x.shape[0]
    t = COMBINE_T
    in_specs = [
        pl.BlockSpec((t, D_MODEL), lambda i, d: (i, 0)),
        pl.BlockSpec((t, LANES), lambda i, d: (i, 0)),
        pl.BlockSpec(memory_space=pl.ANY),
    ]
    args = [dest, x, meta, ys]
    if final_g is not None:
        in_specs.append(pl.BlockSpec((1, D_MODEL), lambda i, d: (0, 0)))
        args.append(final_g)
    return pl.pallas_call(
        functools.partial(_combine_body, final_norm=final_g is not None),
        grid_spec=pltpu.PrefetchScalarGridSpec(
            num_scalar_prefetch=1,
            grid=(s // t,),
            in_specs=in_specs,
            out_specs=pl.BlockSpec((t, D_MODEL), lambda i, d: (i, 0)),
            scratch_shapes=[pltpu.VMEM((2, 2, t) + TOKEN_TILE, F32), pltpu.SemaphoreType.DMA((2,))],
        ),
        out_shape=jax.ShapeDtypeStruct((s, D_MODEL), F32),
        compiler_params=pltpu.CompilerParams(
            dimension_semantics=("arbitrary",), vmem_limit_bytes=VMEM_LIMIT_BYTES),
        name="combine",
    )(*args)


def _moe_ffn(x, ln_g, router_w, router_b, wg, wu, wd, final_g):
    s = x.shape[0]
    tm = FFN_TM
    rw_pad = jnp.zeros((D_MODEL, LANES), F32).at[:, :N_EXPERTS].set(router_w)
    rb_pad = jnp.zeros((1, LANES), F32).at[0, :N_EXPERTS].set(router_b)
    h, meta, counts = _router(x, ln_g, rw_pad, rb_pad)
    idx = meta[:, 0:2].astype(jnp.int32)
    rank = meta[:, 4:6].astype(jnp.int32)
    cnt = counts[0, :N_EXPERTS].astype(jnp.int32)
    padded = (cnt + tm - 1) // tm * tm
    ends = jnp.cumsum(padded)
    dest = ((ends - padded)[idx] + rank).reshape(-1)
    n_tiles = 2 * s // tm + N_EXPERTS
    tile_start = jnp.arange(n_tiles, dtype=jnp.int32) * tm
    tile_expert = jnp.minimum(
        jnp.sum(tile_start[:, None] >= ends[None, :], axis=1), N_EXPERTS - 1).astype(jnp.int32)
    n_used = (ends[-1:] // tm).astype(jnp.int32)
    tile_valid = jnp.clip((ends - padded + cnt)[tile_expert] - tile_start, 0, tm).astype(jnp.int32)
    spare = ends[-1] + jnp.arange(N_EXPERTS, dtype=jnp.int32) * tm
    fill_start = jnp.concatenate([ends - tm, spare]).astype(jnp.int32)
    fill_on = jnp.concatenate([padded > 0, spare < n_tiles * tm]).astype(jnp.int32)
    fill_start = jnp.where(fill_on > 0, fill_start, 0)
    hs = _gather_rows(dest, fill_start, fill_on, h, n_tiles * tm)
    ys = _expert_ffn(tile_expert, n_used, tile_valid, hs, wg, wu, wd)
    return _combine(dest, x, meta, ys, final_g)


def kernel(x, ln_mix_g, w_in, attn_sinks, conv_w, pool_w, pool_scale, sgu_w, sgu_b, w_branch, w_out,
           ln_ffn_g, dense_w_gate, dense_w_up, dense_w_down, router_w, router_b,
           moe_w_gate, moe_w_up, moe_w_down, ln_final_g):
    batch, seq, _ = x.shape
    depth = w_in.shape[0]
    assert batch == 1 and seq % MIX_TM == 0 and seq % FFN_TM == 0 and depth % 2 == 0
    assert D_MODEL == SUBLANES * LANES
    xs = x.reshape(seq, D_MODEL)
    w_in, pool_w, sgu_w, w_branch, w_out = (w.astype(BF16) for w in (w_in, pool_w, sgu_w, w_branch, w_out))
    dense_w = [w.astype(BF16) for w in (dense_w_gate, dense_w_up, dense_w_down)]
    n_moe = moe_w_gate.shape[0]
    moe_w_f32 = [w.reshape(n_moe, -1, w.shape[-1]) for w in (moe_w_gate, moe_w_up, moe_w_down)]
    for layer in range(depth):
        sgu_b_full = jnp.broadcast_to(sgu_b[layer][:, :, None], (4, GROUP, GROUP))
        xs = _mixer(layer, xs, ln_mix_g[layer][None], w_in, attn_sinks[layer], conv_w[layer], pool_w,
                    pool_scale[layer][None], sgu_w, sgu_b_full, w_branch, w_out)
        i = layer // 2
        if layer % 2 == 0:
            xs, moe_w = _dense_ffn(i, xs, ln_ffn_g[layer][None], *dense_w, i, moe_w_f32)
            moe_w = [w.reshape(N_EXPERTS, -1, w.shape[-1]) for w in moe_w]
        else:
            final_g = ln_final_g[None] if layer == depth - 1 else None
            xs = _moe_ffn(xs, ln_ffn_g[layer][None], router_w[i], router_b[i], *moe_w, final_g)
    return xs.reshape(batch, seq, D_MODEL)
```
